```python
import math
import jax, jax.numpy as jnp
from jax import lax
import numpy as np

D_MODEL = 2048
BATCH = 4
SEQ = 2048
DEPTH = 1

D_RNN = 2048
RNN_BLOCKS = 16
RNN_BLOCK_W = D_RNN // RNN_BLOCKS
CONV_W = 4
LRU_C = 8.0
HEAD_DIM = 128
HEADS_PER_GROUP = 8
DILATED_GROUPS = ((128, 1), (512, 4), (2048, 16))
N_GROUPS = len(DILATED_GROUPS)
N_ATTN_HEADS = N_GROUPS * HEADS_PER_GROUP
D_QKV = N_ATTN_HEADS * HEAD_DIM
D_ATTN_OUT = HEADS_PER_GROUP * HEAD_DIM
ATTN_BLOCK = 128
ROPE_THETA = 500000.0
ROT_DIM = HEAD_DIM // 4
NORM_EPS = 1e-6
SPLIT_SIZES = (D_RNN, D_RNN, D_QKV, D_QKV, D_QKV, D_ATTN_OUT, D_MODEL, D_MODEL)
D_IN_TOTAL = 2 * D_RNN + 3 * D_QKV + D_ATTN_OUT + 2 * D_MODEL

kernel_name = "hybrid_rglru_dilated_attn_block"


def rms_norm(x, g):
    xf = x.astype(jnp.float32)
    y = xf * lax.rsqrt(jnp.mean(xf * xf, axis=-1, keepdims=True) + NORM_EPS)
    return (y * g.astype(jnp.float32)).astype(x.dtype)


def split_columns(p):
    offs, acc = [], 0
    for s in SPLIT_SIZES[:-1]:
        acc += s
        offs.append(acc)
    return jnp.split(p, offs, axis=-1)


def partial_rope(t, pos):
    half = ROT_DIM // 2
    inv = jnp.exp(-math.log(ROPE_THETA) * jnp.arange(half, dtype=jnp.float32) * (2.0 / ROT_DIM))
    ang = pos.astype(jnp.float32)[:, None] * inv[None, :]
    cos = jnp.cos(ang)[None, :, None, :]
    sin = jnp.sin(ang)[None, :, None, :]
    tr = t[..., :ROT_DIM].astype(jnp.float32)
    t1, t2 = tr[..., :half], tr[..., half:]
    rot = jnp.concatenate([t1 * cos - t2 * sin, t2 * cos + t1 * sin], axis=-1)
    return jnp.concatenate([rot.astype(t.dtype), t[..., ROT_DIM:]], axis=-1)


def rg_lru(u, conv_w, conv_b, w_rg, b_rg, w_ig, b_ig, lru_lambda):
    B, S, C = u.shape
    xc = lax.conv_general_dilated(
        u, conv_w[:, None, :].astype(u.dtype), window_strides=(1,), padding=[(CONV_W - 1, 0)],
        dimension_numbers=("NWC", "WIO", "NWC"), feature_group_count=C) + conv_b
    xb = xc.reshape(B, S, RNN_BLOCKS, RNN_BLOCK_W)
    r = jax.nn.sigmoid(jnp.einsum('bsnc,ncd->bsnd', xb, w_rg).reshape(B, S, C) + b_rg)
    i = jax.nn.sigmoid(jnp.einsum('bsnc,ncd->bsnd', xb, w_ig).reshape(B, S, C) + b_ig)
    log_a = -LRU_C * r.astype(jnp.float32) * jax.nn.softplus(-lru_lambda.astype(jnp.float32))
    a = jnp.exp(log_a)
    b = jnp.sqrt(-jnp.expm1(2.0 * log_a)) * (i * xc).astype(jnp.float32)

    def combine(left, right):
        a1, b1 = left
        a2, b2 = right
        return a1 * a2, a2 * b1 + b2

    _, h = lax.associative_scan(combine, (a, b), axis=1)
    return h.astype(u.dtype)


def dilated_group_attention(q, k, v, window, dilation):
    B, S, H, D = q.shape
    L = S // dilation
    steps = window // dilation
    n_blk = -(-L // ATTN_BLOCK)
    Lp = n_blk * ATTN_BLOCK

    def to_blocks(t):
        t = t.reshape(B, L, dilation, H, D).transpose(0, 2, 1, 3, 4)
        t = jnp.pad(t, ((0, 0), (0, 0), (0, Lp - L), (0, 0), (0, 0)))
        return t.reshape(B, dilation, n_blk, ATTN_BLOCK, H, D)

    def with_prev(t):
        prev = jnp.pad(t, ((0, 0), (0, 0), (1, 0), (0, 0), (0, 0), (0, 0)))[:, :, :-1]
        return jnp.concatenate([prev, t], axis=3)

    qb = to_blocks(q).astype(jnp.float32)
    kk = with_prev(to_blocks(k)).astype(jnp.float32)
    vv = with_prev(to_blocks(v)).astype(jnp.float32)
    s = jnp.einsum('brnqhd,brnkhd->brnhqk', qb, kk) * (D ** -0.5)
    qi = jnp.arange(ATTN_BLOCK)[:, None]
    kj = jnp.arange(2 * ATTN_BLOCK)[None, :]
    dist = qi + ATTN_BLOCK - kj
    key_pos = jnp.arange(n_blk)[:, None, None] * ATTN_BLOCK + kj[None] - ATTN_BLOCK
    valid = (dist >= 0)[None] & (dist <= steps)[None] & (key_pos >= 0)
    s = jnp.where(valid[None, None, :, None], s, -1e30)
    m = jnp.max(s, axis=-1)
    p = jnp.exp(s - m[..., None])
    l = jnp.sum(p, axis=-1)
    o = jnp.einsum('brnhqk,brnkhd->brnqhd', p, vv) / jnp.swapaxes(l, -1, -2)[..., None]
    log_den = jnp.swapaxes(m + jnp.log(l), -1, -2)

    def from_blocks(t):
        rest = t.shape[4:]
        t = t.reshape((B, dilation, Lp) + rest)[:, :, :L]
        t = jnp.moveaxis(t, 1, 2)
        return t.reshape((B, S) + rest)

    return from_blocks(o), from_blocks(log_den)


def setup_inputs(seed: int = 0) -> dict:
    key = jax.random.key(seed)
    ks = jax.random.split(key, 16)
    f32 = jnp.float32
    nrm = lambda k, shape, scale: jax.random.normal(k, shape, f32) * scale
    u = jax.random.uniform(ks[8], (DEPTH, D_RNN), f32, 0.9, 0.999)
    s = u ** (1.0 / LRU_C)
    lru_lambda = jnp.log(s) - jnp.log1p(-s)
    return {
        "x": jax.random.normal(ks[0], (BATCH, SEQ, D_MODEL), f32),
        "ln_pre_g": 1.0 + nrm(ks[1], (DEPTH, D_MODEL), 0.02),
        "w_in": nrm(ks[2], (DEPTH, D_MODEL, D_IN_TOTAL), D_MODEL ** -0.5),
        "conv_w": nrm(ks[3], (DEPTH, CONV_W, D_RNN), CONV_W ** -0.5),
        "conv_b": nrm(ks[4], (DEPTH, D_RNN), 0.02),
        "w_rg": nrm(ks[5], (DEPTH, RNN_BLOCKS, RNN_BLOCK_W, RNN_BLOCK_W), RNN_BLOCK_W ** -0.5),
        "b_rg": nrm(ks[6], (DEPTH, D_RNN), 0.02),
        "w_ig": nrm(ks[7], (DEPTH, RNN_BLOCKS, RNN_BLOCK_W, RNN_BLOCK_W), RNN_BLOCK_W ** -0.5),
        "b_ig": nrm(ks[9], (DEPTH, D_RNN), 0.02),
        "lru_lambda": lru_lambda,
        "w_rnn_out": nrm(ks[10], (DEPTH, D_RNN, D_MODEL), D_RNN ** -0.5),
        "w_attn_out": nrm(ks[11], (DEPTH, D_ATTN_OUT, D_MODEL), D_ATTN_OUT ** -0.5),
        "w_o": nrm(ks[12], (DEPTH, D_MODEL, D_MODEL), D_MODEL ** -0.5),
        "ln_post_g": 1.0 + nrm(ks[13], (DEPTH, D_MODEL), 0.02),
    }


def reference(x, ln_pre_g, w_in, conv_w, conv_b, w_rg, b_rg, w_ig, b_ig, lru_lambda,
              w_rnn_out, w_attn_out, w_o, ln_post_g):
    B, S, _ = x.shape
    pos = jnp.arange(S)
    for layer in range(DEPTH):
        h = rms_norm(x, ln_pre_g[layer])
        proj = jnp.einsum('bsd,de->bse', h, w_in[layer])
        rnn_x, rnn_gate, q, k, v, attn_gate, g_rnn, g_attn = split_columns(proj)

        hr = rg_lru(rnn_x, conv_w[layer], conv_b[layer], w_rg[layer], b_rg[layer],
                    w_ig[layer], b_ig[layer], lru_lambda[layer])
        y_rnn = jnp.einsum('bsc,cd->bsd', hr * jax.nn.silu(rnn_gate), w_rnn_out[layer])

        q = partial_rope(q.reshape(B, S, N_ATTN_HEADS, HEAD_DIM), pos)
        k = partial_rope(k.reshape(B, S, N_ATTN_HEADS, HEAD_DIM), pos)
        v = v.reshape(B, S, N_ATTN_HEADS, HEAD_DIM)
        q = q.reshape(B, S, N_GROUPS, HEADS_PER_GROUP, HEAD_DIM)
        k = k.reshape(B, S, N_GROUPS, HEADS_PER_GROUP, HEAD_DIM)
        v = v.reshape(B, S, N_GROUPS, HEADS_PER_GROUP, HEAD_DIM)
        outs, dens = [], []
        for g, (window, dilation) in enumerate(DILATED_GROUPS):
            o_g, d_g = dilated_group_attention(q[:, :, g], k[:, :, g], v[:, :, g], window, dilation)
            outs.append(o_g)
            dens.append(d_g)
        alpha = jax.nn.softmax(jnp.stack(dens, axis=0), axis=0)
        o = jnp.sum(alpha[..., None] * jnp.stack(outs, axis=0), axis=0)
        o = o.reshape(B, S, D_ATTN_OUT).astype(x.dtype)
        y_attn = jnp.einsum('bsc,cd->bsd', o * jax.nn.silu(attn_gate), w_attn_out[layer])

        merged = jax.nn.sigmoid(g_rnn) * y_rnn + jax.nn.sigmoid(g_attn) * y_attn
        y = jnp.einsum('bsd,de->bse', merged, w_o[layer])
        x = x + rms_norm(y, ln_post_g[layer])
    return x
```

```python
import functools
import math

import jax
import jax.numpy as jnp
from jax import lax
from jax.experimental import pallas as pl
from jax.experimental.pallas import tpu as pltpu

D_MODEL = 2048
D_RNN = 2048
RNN_BLOCKS = 16
RNN_BLOCK_W = 128
CONV_W = 4
LRU_C = 8.0
HEAD_DIM = 128
HEADS_PER_GROUP = 8
DILATIONS = (1, 4, 16)
ATTN_BLOCK = 128
ROPE_THETA = 500000.0
ROT_DIM = 32
NORM_EPS = 1e-6
MASK_VALUE = -1e30

BF16 = jnp.bfloat16
F32 = jnp.float32

_COL_BLK = 1024
_Q_BLK0 = 4
_NAT_BLOCKS = 9
_P_GATE_BLK128 = 32
_P_GRNN_BLK = 5
_P_GATTN_BLK = 7

_VMEM_LIMIT = 56 * 1024 * 1024


def _cparams(n_axes):
    return pltpu.CompilerParams(
        dimension_semantics=("arbitrary",) * n_axes, vmem_limit_bytes=_VMEM_LIMIT)


_NORM_T = 256


def _prenorm_kernel(x_ref, g_ref, h0_ref, h4_ref, h16_ref, y_scr):
    x = x_ref[0]
    ms = jnp.mean(x * x, axis=-1, keepdims=True)
    y = x * lax.rsqrt(ms + NORM_EPS) * g_ref[...]
    h0_ref[0] = y.astype(BF16)
    for c in range(y_scr.shape[0]):
        lanes = slice(c * 128, (c + 1) * 128)
        y_scr[c] = y[:, lanes]
        for r in range(4):
            h4_ref[0, r, :, lanes] = y_scr[c, pl.ds(r, _NORM_T // 4, stride=4), :].astype(BF16)
        for r in range(16):
            h16_ref[0, r, :, lanes] = y_scr[c, pl.ds(r, _NORM_T // 16, stride=16), :].astype(BF16)


def _prenorm(x, g):
    B, S, D = x.shape
    T = _NORM_T
    return pl.pallas_call(
        _prenorm_kernel,
        grid=(B, S // T),
        in_specs=[pl.BlockSpec((1, T, D), lambda b, t: (b, t, 0)),
                  pl.BlockSpec((1, D), lambda b, t: (0, 0))],
        out_specs=[pl.BlockSpec((1, T, D), lambda b, t: (b, t, 0)),
                   pl.BlockSpec((1, 4, T // 4, D), lambda b, t: (b, 0, t, 0)),
                   pl.BlockSpec((1, 16, T // 16, D), lambda b, t: (b, 0, t, 0))],
        out_shape=[jax.ShapeDtypeStruct((B, S, D), BF16),
                   jax.ShapeDtypeStruct((B, 4, S // 4, D), BF16),
                   jax.ShapeDtypeStruct((B, 16, S // 16, D), BF16)],
        scratch_shapes=[pltpu.VMEM((D // 128, T, 128), F32)],
        compiler_params=_cparams(2),
        name="prenorm",
    )(x, g.reshape(1, D))


def _cast_weight_once(w_ref, wb_ref):
    @pl.when(pl.program_id(1) == 0)
    def _():
        wb_ref[...] = w_ref[...].astype(BF16)


def _proj_nat_kernel(lhs_ref, w_ref, o_ref, wb_ref):
    _cast_weight_once(w_ref, wb_ref)
    o_ref[...] = jnp.dot(lhs_ref[...], wb_ref[...], preferred_element_type=F32)


def _proj_nat(h, w_in):
    M, K = h.shape
    tm, tn = 1024, _COL_BLK
    return pl.pallas_call(
        _proj_nat_kernel,
        grid=(_NAT_BLOCKS, M // tm),
        in_specs=[pl.BlockSpec((tm, K), lambda j, i: (i, 0)),
                  pl.BlockSpec((K, tn), lambda j, i: (0, jnp.where(j < 4, j, j + 9)))],
        out_specs=pl.BlockSpec((tm, tn), lambda j, i: (i, j)),
        out_shape=jax.ShapeDtypeStruct((M, _NAT_BLOCKS * tn), F32),
        scratch_shapes=[pltpu.VMEM((K, tn), BF16)],
        compiler_params=_cparams(2),
        name="proj_nat",
    )(h, w_in)


def _proj_qkv_kernel(lhs_ref, w_ref, cos_ref, sa_ref, sb_ref, o_ref, wb_ref):
    _cast_weight_once(w_ref, wb_ref)
    acc = jnp.dot(lhs_ref[...], wb_ref[...], preferred_element_type=F32)
    j = pl.program_id(0)

    @pl.when(j < 2)
    def _():
        cos, sa, sb = cos_ref[...], sa_ref[...], sb_ref[...]
        for c in range(HEADS_PER_GROUP):
            t = acc[:, c * HEAD_DIM:(c + 1) * HEAD_DIM]
            rot = (t * cos + pltpu.roll(t, HEAD_DIM - ROT_DIM // 2, 1) * sa
                   + pltpu.roll(t, ROT_DIM // 2, 1) * sb)
            o_ref[0, 0, c] = rot.astype(BF16)

    @pl.when(j == 2)
    def _():
        for c in range(HEADS_PER_GROUP):
            o_ref[0, 0, c] = acc[:, c * HEAD_DIM:(c + 1) * HEAD_DIM].astype(BF16)


def _proj_qkv(h_perm, w_in, tables, group, B, S):
    M, K = h_perm.shape
    tm, tn = 1024, _COL_BLK
    spt = S // tm
    tbl_spec = pl.BlockSpec((tm, HEAD_DIM), lambda j, i: (i % spt, 0))
    return pl.pallas_call(
        _proj_qkv_kernel,
        grid=(3, M // tm),
        in_specs=[pl.BlockSpec((tm, K), lambda j, i: (i, 0)),
                  pl.BlockSpec((K, tn), lambda j, i: (0, _Q_BLK0 + group + 3 * j)),
                  tbl_spec, tbl_spec, tbl_spec],
        out_specs=pl.BlockSpec((1, 1, HEADS_PER_GROUP, tm, HEAD_DIM),
                               lambda j, i: (j, i // spt, 0, i % spt, 0)),
        out_shape=jax.ShapeDtypeStruct((3, B, HEADS_PER_GROUP, S, HEAD_DIM), BF16),
        scratch_shapes=[pltpu.VMEM((K, tn), BF16)],
        compiler_params=_cparams(2),
        name=f"proj_qkv_g{group}",
    )(h_perm, w_in, *tables)


def _rope_tables(S, dilation):
    half = ROT_DIM // 2
    inv = jnp.exp(-math.log(ROPE_THETA) * jnp.arange(half, dtype=F32) * (2.0 / ROT_DIM))
    L = S // dilation
    p = jnp.arange(S)
    pos = (p % L) * dilation + p // L
    ang = pos.astype(F32)[:, None] * inv[None, :]
    cos, sin = jnp.cos(ang), jnp.sin(ang)
    z16 = jnp.zeros((S, half), F32)
    rest0 = jnp.zeros((S, HEAD_DIM - ROT_DIM), F32)
    cos_t = jnp.concatenate([cos, cos, jnp.ones((S, HEAD_DIM - ROT_DIM), F32)], axis=1)
    sin_a = jnp.concatenate([-sin, z16, rest0], axis=1)
    sin_b = jnp.concatenate([z16, sin, rest0], axis=1)
    return cos_t, sin_a, sin_b


_RNN_TS = 256
_PAD = 8


def _softplus(z):
    return jnp.maximum(z, 0.0) + jnp.log1p(jnp.exp(-jnp.abs(z)))


def _scan8(a, b, row):
    for k in (1, 2, 4):
        a_sh = jnp.where(row >= k, pltpu.roll(a, k, 0), 1.0)
        b_sh = jnp.where(row >= k, pltpu.roll(b, k, 0), 0.0)
        b = a * b_sh + b
        a = a * a_sh
    return a, b


def _rglru_kernel(x_ref, gate_ref, cw_ref, cb_ref, wrg_ref, brg_ref, wig_ref, big_ref, lam_ref,
                  o_ref, xpad, a_scr, b_scr, carry):
    ts = _RNN_TS
    C = D_RNN

    @pl.when(pl.program_id(1) == 0)
    def _():
        xpad[0:_PAD, :] = jnp.zeros((_PAD, C), F32)
        carry[...] = jnp.zeros((1, C), F32)

    xpad[_PAD:_PAD + ts, :] = x_ref[0]

    def gates(n, _):
        cs = pl.ds(pl.multiple_of(n * RNN_BLOCK_W, RNN_BLOCK_W), RNN_BLOCK_W)
        xc = cb_ref[:, cs]
        for k in range(CONV_W):
            xc = xc + cw_ref[k:k + 1, cs] * xpad[pl.ds(_PAD - (CONV_W - 1) + k, ts), cs]
        xb = xc.astype(BF16)
        r = jax.nn.sigmoid(
            jnp.dot(xb, wrg_ref[n].astype(BF16), preferred_element_type=F32) + brg_ref[:, cs])
        i = jax.nn.sigmoid(
            jnp.dot(xb, wig_ref[n].astype(BF16), preferred_element_type=F32) + big_ref[:, cs])
        log_a = (-LRU_C * _softplus(-lam_ref[:, cs])) * r
        a = jnp.exp(log_a)
        a_scr[:, cs] = a
        b_scr[:, cs] = jnp.sqrt(1.0 - a * a) * (i * xc)
        return 0

    lax.fori_loop(0, RNN_BLOCKS, gates, 0)
    xpad[0:_PAD, :] = x_ref[0, ts - _PAD:ts, :]

    row = lax.broadcasted_iota(jnp.int32, (8, C), 0)

    def slab_pair(s, h):
        base = pl.multiple_of(s * 16, 16)
        hs = []
        for half in range(2):
            rows = pl.ds(base + 8 * half, 8)
            a, b = _scan8(a_scr[rows, :], b_scr[rows, :], row)
            hh = a * h + b
            h = hh[7:8, :]
            hs.append(hh)
        g = gate_ref[0, pl.ds(base, 16), :]
        o_ref[0, pl.ds(base, 16), :] = (
            jnp.concatenate(hs, axis=0) * (g * jax.nn.sigmoid(g))).astype(BF16)
        return h

    carry[...] = lax.fori_loop(0, ts // 16, slab_pair, carry[...])


def _rglru(p_nat, conv_w, conv_b, w_rg, b_rg, w_ig, b_ig, lam, B, S):
    ts, C = _RNN_TS, D_RNN
    row_spec = lambda: pl.BlockSpec((1, C), lambda b, t: (0, 0))
    blk_spec = lambda: pl.BlockSpec((RNN_BLOCKS, RNN_BLOCK_W, RNN_BLOCK_W), lambda b, t: (0, 0, 0))
    return pl.pallas_call(
        _rglru_kernel,
        grid=(B, S // ts),
        in_specs=[pl.BlockSpec((1, ts, C), lambda b, t: (b, t, 0)),
                  pl.BlockSpec((1, ts, C), lambda b, t: (b, t, 1)),
                  pl.BlockSpec((CONV_W, C), lambda b, t: (0, 0)),
                  row_spec(), blk_spec(), row_spec(), blk_spec(), row_spec(), row_spec()],
        out_specs=pl.BlockSpec((1, ts, C), lambda b, t: (b, t, 0)),
        out_shape=jax.ShapeDtypeStruct((B, S, C), BF16),
        scratch_shapes=[pltpu.VMEM((ts + _PAD, C), F32), pltpu.VMEM((ts, C), F32),
                        pltpu.VMEM((ts, C), F32), pltpu.VMEM((1, C), F32)],
        compiler_params=_cparams(2),
        name="rglru",
    )(p_nat, p_nat, conv_w, conv_b.reshape(1, C), w_rg, b_rg.reshape(1, C), w_ig,
      b_ig.reshape(1, C), lam.reshape(1, C))


def _attend(q, k, v, has_prev):
    nk = k.shape[0]
    s = lax.dot_general(q, k, (((1,), (1,)), ((), ())), preferred_element_type=F32)
    s = s * (HEAD_DIM ** -0.5)
    qi = lax.broadcasted_iota(jnp.int32, (ATTN_BLOCK, nk), 0)
    kj = lax.broadcasted_iota(jnp.int32, (ATTN_BLOCK, nk), 1)
    if has_prev:
        d = kj - qi
        valid = (d >= 0) & (d <= ATTN_BLOCK)
    else:
        valid = kj <= qi
    s = jnp.where(valid, s, MASK_VALUE)
    m = jnp.max(s, axis=-1, keepdims=True)
    p = jnp.exp(s - m)
    l = jnp.sum(p, axis=-1, keepdims=True)
    o = jnp.dot(p.astype(BF16), v, preferred_element_type=F32) * (1.0 / l)
    return o, m + jnp.log(l)


def _attn_kernel(g0_ref, g1_ref, g2_ref, gate_ref, out_ref, o_scr, ld_scr):
    S = o_scr.shape[1]
    blk = ATTN_BLOCK

    def run_block(ref, base, has_prev):
        lo = base - blk if has_prev else base
        q = ref[0, 0, 0, base:base + blk, :]
        k = ref[1, 0, 0, lo:base + blk, :]
        v = ref[2, 0, 0, lo:base + blk, :]
        o, ld = _attend(q, k, v, has_prev)
        return o, jnp.broadcast_to(ld, (blk, HEAD_DIM))

    for n in range(S // blk):
        o, ld = run_block(g0_ref, n * blk, n > 0)
        o_scr[0, n * blk:(n + 1) * blk, :] = o
        ld_scr[0, n * blk:(n + 1) * blk, :] = ld

    for gi, (ref, dil) in enumerate(((g1_ref, DILATIONS[1]), (g2_ref, DILATIONS[2])), start=1):
        L = S // dil
        for r in range(dil):
            for n in range(L // blk):
                o, ld = run_block(ref, r * L + n * blk, n > 0)
                rows = pl.ds(n * blk * dil + r, blk, stride=dil)
                o_scr[gi, rows, :] = o
                ld_scr[gi, rows, :] = ld

    ch = 256

    def merge(c, _):
        rows = pl.ds(pl.multiple_of(c * ch, ch), ch)
        l0, l1, l2 = ld_scr[0, rows, :], ld_scr[1, rows, :], ld_scr[2, rows, :]
        m = jnp.maximum(jnp.maximum(l0, l1), l2)
        e0, e1, e2 = jnp.exp(l0 - m), jnp.exp(l1 - m), jnp.exp(l2 - m)
        num = e0 * o_scr[0, rows, :] + e1 * o_scr[1, rows, :] + e2 * o_scr[2, rows, :]
        o = num * (1.0 / (e0 + e1 + e2))
        g = gate_ref[0, rows, :]
        out_ref[0, rows, :] = (o * (g * jax.nn.sigmoid(g))).astype(BF16)
        return 0

    lax.fori_loop(0, S // ch, merge, 0)


def _attention(qkv, p_nat, B, S):
    H = HEADS_PER_GROUP
    qkv_spec = lambda: pl.BlockSpec((3, 1, 1, S, HEAD_DIM), lambda b, h: (0, b, h, 0, 0))
    return pl.pallas_call(
        _attn_kernel,
        grid=(B, H),
        in_specs=[qkv_spec(), qkv_spec(), qkv_spec(),
                  pl.BlockSpec((1, S, HEAD_DIM), lambda b, h: (b, 0, _P_GATE_BLK128 + h))],
        out_specs=pl.BlockSpec((1, S, HEAD_DIM), lambda b, h: (b, 0, h)),
        out_shape=jax.ShapeDtypeStruct((B, S, H * HEAD_DIM), BF16),
        scratch_shapes=[pltpu.VMEM((3, S, HEAD_DIM), F32), pltpu.VMEM((3, S, HEAD_DIM), F32)],
        compiler_params=_cparams(2),
        name="dilated_attn",
    )(*qkv, p_nat)


def _out_rnn_kernel(lhs_ref, w_ref, g_ref, o_ref, wb_ref):
    _cast_weight_once(w_ref, wb_ref)
    y = jnp.dot(lhs_ref[...], wb_ref[...], preferred_element_type=F32)
    o_ref[...] = jax.nn.sigmoid(g_ref[...]) * y


def _out_attn_kernel(lhs_ref, w_ref, g_ref, m_ref, o_ref, wb_ref):
    _cast_weight_once(w_ref, wb_ref)
    y = jnp.dot(lhs_ref[...], wb_ref[...], preferred_element_type=F32)
    o_ref[...] = (m_ref[...] + jax.nn.sigmoid(g_ref[...]) * y).astype(BF16)


def _gated_matmul(kernel, lhs, w, extras, extra_blk0, out_dtype, tm=512, tn=1024):
    M, K = lhs.shape
    N = w.shape[1]
    extra_specs = [pl.BlockSpec((tm, tn), functools.partial(lambda j, i, o: (i, o + j), o=o))
                   for o in extra_blk0]
    return pl.pallas_call(
        kernel,
        grid=(N // tn, M // tm),
        in_specs=[pl.BlockSpec((tm, K), lambda j, i: (i, 0)),
                  pl.BlockSpec((K, tn), lambda j, i: (0, j))] + extra_specs,
        out_specs=pl.BlockSpec((tm, tn), lambda j, i: (i, j)),
        out_shape=jax.ShapeDtypeStruct((M, N), out_dtype),
        scratch_shapes=[pltpu.VMEM((K, tn), BF16)],
        compiler_params=_cparams(2),
        name=kernel.__name__.strip("_"),
    )(lhs, w, *extras)


def _plain_mm_kernel(lhs_ref, w_ref, o_ref, wb_ref):
    _cast_weight_once(w_ref, wb_ref)
    o_ref[...] = jnp.dot(lhs_ref[...], wb_ref[...], preferred_element_type=F32)


def _postnorm_kernel(x_ref, y_ref, g_ref, o_ref):
    y = y_ref[...]
    ms = jnp.mean(y * y, axis=-1, keepdims=True)
    o_ref[...] = x_ref[...] + y * lax.rsqrt(ms + NORM_EPS) * g_ref[...]


def _postnorm(x2d, y, g):
    M, D = x2d.shape
    tm = 512
    return pl.pallas_call(
        _postnorm_kernel,
        grid=(M // tm,),
        in_specs=[pl.BlockSpec((tm, D), lambda i: (i, 0)),
                  pl.BlockSpec((tm, D), lambda i: (i, 0)),
                  pl.BlockSpec((1, D), lambda i: (0, 0))],
        out_specs=pl.BlockSpec((tm, D), lambda i: (i, 0)),
        out_shape=jax.ShapeDtypeStruct((M, D), F32),
        compiler_params=_cparams(1),
        name="postnorm",
    )(x2d, y, g.reshape(1, D))


def _layer(x, ln_pre_g, w_in, conv_w, conv_b, w_rg, b_rg, w_ig, b_ig, lru_lambda,
           w_rnn_out, w_attn_out, w_o, ln_post_g):
    B, S, D = x.shape
    M = B * S
    h0, h4, h16 = _prenorm(x, ln_pre_g)
    p_nat = _proj_nat(h0.reshape(M, D), w_in)
    qkv = [
        _proj_qkv(h.reshape(M, D), w_in, _rope_tables(S, dil), g, B, S)
        for g, (h, dil) in enumerate(zip((h0, h4, h16), DILATIONS))
    ]
    p3 = p_nat.reshape(B, S, p_nat.shape[1])
    u_rnn = _rglru(p3, conv_w, conv_b, w_rg, b_rg, w_ig, b_ig, lru_lambda, B, S)
    u_attn = _attention(qkv, p3, B, S)
    m_rnn = _gated_matmul(_out_rnn_kernel, u_rnn.reshape(M, D_RNN), w_rnn_out,
                          [p_nat], [_P_GRNN_BLK], F32)
    merged = _gated_matmul(_out_attn_kernel, u_attn.reshape(M, -1), w_attn_out,
                           [p_nat, m_rnn], [_P_GATTN_BLK, 0], BF16)
    y = _gated_matmul(_plain_mm_kernel, merged, w_o, [], [], F32)
    return _postnorm(x.reshape(M, D), y, ln_post_g).reshape(B, S, D)


def kernel(x, ln_pre_g, w_in, conv_w, conv_b, w_rg, b_rg, w_ig, b_ig, lru_lambda,
           w_rnn_out, w_attn_out, w_o, ln_post_g):
    for layer in range(ln_pre_g.shape[0]):
        x = _layer(x, ln_pre_g[layer], w_in[layer], conv_w[layer], conv_b[layer], w_rg[layer],
                   b_rg[layer], w_ig[layer], b_ig[layer], lru_lambda[layer], w_rnn_out[layer],
                   w_attn_out[layer], w_o[layer], ln_post_g[layer])
    return x
```

```python
import functools
import math

import jax
import jax.numpy as jnp
import numpy as np
from jax import lax
from jax.experimental import pallas as pl
from jax.experimental.pallas import tpu as pltpu

D_MODEL = 2048
D_RNN = 2048
RNN_BLOCKS = 16
RNN_BLOCK_W = 128
CONV_W = 4
LRU_C = 8.0
HEAD_DIM = 128
HEADS_PER_GROUP = 8
DILATIONS = (1, 4, 16)
ATTN_BLOCK = 128
ROPE_THETA = 500000.0
ROT_DIM = 32
NORM_EPS = 1e-6
MASK_VALUE = -1e30

BF16 = jnp.bfloat16
F32 = jnp.float32

_COL_BLK = 1024
_Q_BLK0 = 4
_NAT_BLOCKS = 9
_P_GRNN_BLK = 4
_P_GATTN_BLK = 6
_P_GATE_BLK = 8


def _nat_w_block(j):
    return jnp.where(j < 4, j, jnp.where(j < 8, j + 10, 13))

_HEADS_PER_DOT = 2

_VMEM_LIMIT = 56 * 1024 * 1024


def _cparams(n_axes):
    return pltpu.CompilerParams(
        dimension_semantics=("arbitrary",) * n_axes, vmem_limit_bytes=_VMEM_LIMIT)


_NORM_T = 256


def _prenorm_kernel(x_ref, g_ref, h0_ref, h4_ref, h16_ref, y_scr):
    x = x_ref[0]
    ms = jnp.mean(x * x, axis=-1, keepdims=True)
    y = x * lax.rsqrt(ms + NORM_EPS) * g_ref[...]
    h0_ref[0] = y.astype(BF16)
    for c in range(y_scr.shape[0]):
        lanes = slice(c * 128, (c + 1) * 128)
        y_scr[c] = y[:, lanes]
        for r in range(4):
            h4_ref[0, r, :, lanes] = y_scr[c, pl.ds(r, _NORM_T // 4, stride=4), :].astype(BF16)
        for r in range(16):
            h16_ref[0, r, :, lanes] = y_scr[c, pl.ds(r, _NORM_T // 16, stride=16), :].astype(BF16)


def _prenorm(x, g):
    B, S, D = x.shape
    T = _NORM_T
    return pl.pallas_call(
        _prenorm_kernel,
        grid=(B, S // T),
        in_specs=[pl.BlockSpec((1, T, D), lambda b, t: (b, t, 0)),
                  pl.BlockSpec((1, D), lambda b, t: (0, 0))],
        out_specs=[pl.BlockSpec((1, T, D), lambda b, t: (b, t, 0)),
                   pl.BlockSpec((1, 4, T // 4, D), lambda b, t: (b, 0, t, 0)),
                   pl.BlockSpec((1, 16, T // 16, D), lambda b, t: (b, 0, t, 0))],
        out_shape=[jax.ShapeDtypeStruct((B, S, D), BF16),
                   jax.ShapeDtypeStruct((B, 4, S // 4, D), BF16),
                   jax.ShapeDtypeStruct((B, 16, S // 16, D), BF16)],
        scratch_shapes=[pltpu.VMEM((D // 128, T, 128), F32)],
        compiler_params=_cparams(2),
        name="prenorm",
    )(x, g.reshape(1, D))


def _cast_weight_once(w_ref, wb_ref):
    @pl.when(pl.program_id(1) == 0)
    def _():
        wb_ref[...] = w_ref[...].astype(BF16)


def _proj_nat_kernel(lhs_ref, w_ref, o_ref, wb_ref):
    _cast_weight_once(w_ref, wb_ref)
    o_ref[...] = jnp.dot(lhs_ref[...], wb_ref[...], preferred_element_type=F32).astype(BF16)


def _proj_nat(h, w_in):
    M, K = h.shape
    tm, tn = 1024, _COL_BLK
    return pl.pallas_call(
        _proj_nat_kernel,
        grid=(_NAT_BLOCKS, M // tm),
        in_specs=[pl.BlockSpec((tm, K), lambda j, i: (i, 0)),
                  pl.BlockSpec((K, tn), lambda j, i: (0, _nat_w_block(j)))],
        out_specs=pl.BlockSpec((tm, tn), lambda j, i: (i, j)),
        out_shape=jax.ShapeDtypeStruct((M, _NAT_BLOCKS * tn), BF16),
        scratch_shapes=[pltpu.VMEM((K, tn), BF16)],
        compiler_params=_cparams(2),
        name="proj_nat",
    )(h, w_in)


def _proj_qkv_kernel(lhs_ref, w_ref, cos_ref, sa_ref, sb_ref, o_ref, wb_ref):
    _cast_weight_once(w_ref, wb_ref)
    is_qk = pl.program_id(0) < 2
    cos = jnp.where(is_qk, cos_ref[...], 1.0)
    sa = jnp.where(is_qk, sa_ref[...], 0.0)
    sb = jnp.where(is_qk, sb_ref[...], 0.0)
    for c0 in range(0, HEADS_PER_GROUP, _HEADS_PER_DOT):
        acc = jnp.dot(lhs_ref[...], wb_ref[:, c0 * HEAD_DIM:(c0 + _HEADS_PER_DOT) * HEAD_DIM],
                      preferred_element_type=F32)
        for c in range(_HEADS_PER_DOT):
            t = acc[:, c * HEAD_DIM:(c + 1) * HEAD_DIM]
            rot = (t * cos + pltpu.roll(t, HEAD_DIM - ROT_DIM // 2, 1) * sa
                   + pltpu.roll(t, ROT_DIM // 2, 1) * sb)
            o_ref[0, 0, c0 + c] = rot.astype(BF16)


def _proj_qkv(h_perm, w_in, tables, group, B, S):
    M, K = h_perm.shape
    tm, tn = 1024, _COL_BLK
    spt = S // tm
    tbl_spec = pl.BlockSpec((tm, HEAD_DIM), lambda j, i: (i % spt, 0))
    return pl.pallas_call(
        _proj_qkv_kernel,
        grid=(3, M // tm),
        in_specs=[pl.BlockSpec((tm, K), lambda j, i: (i, 0)),
                  pl.BlockSpec((K, tn), lambda j, i: (0, _Q_BLK0 + group + 3 * j)),
                  tbl_spec, tbl_spec, tbl_spec],
        out_specs=pl.BlockSpec((1, 1, HEADS_PER_GROUP, tm, HEAD_DIM),
                               lambda j, i: (j, i // spt, 0, i % spt, 0)),
        out_shape=jax.ShapeDtypeStruct((3, B, HEADS_PER_GROUP, S, HEAD_DIM), BF16),
        scratch_shapes=[pltpu.VMEM((K, tn), BF16)],
        compiler_params=_cparams(2),
        name=f"proj_qkv_g{group}",
    )(h_perm, w_in, *tables)


def _rope_tables(S, dilation):
    half = ROT_DIM // 2
    inv = np.exp(-math.log(ROPE_THETA) * np.arange(half, dtype=np.float32)
                 * np.float32(2.0 / ROT_DIM)).astype(np.float32)
    L = S // dilation
    p = np.arange(S)
    pos = (p % L) * dilation + p // L
    ang = pos.astype(np.float32)[:, None] * inv[None, :]
    cos, sin = np.cos(ang).astype(np.float32), np.sin(ang).astype(np.float32)
    cos_t = np.ones((S, HEAD_DIM), np.float32)
    sin_a = np.zeros((S, HEAD_DIM), np.float32)
    sin_b = np.zeros((S, HEAD_DIM), np.float32)
    cos_t[:, :half] = cos
    cos_t[:, half:ROT_DIM] = cos
    sin_a[:, :half] = -sin
    sin_b[:, half:ROT_DIM] = sin
    return jnp.asarray(cos_t), jnp.asarray(sin_a), jnp.asarray(sin_b)


_RNN_TS = 256
_PAD = 8


def _softplus(z):
    return jnp.maximum(z, 0.0) + jnp.log1p(jnp.exp(-jnp.abs(z)))


def _scan8(a, b, row):
    for k in (1, 2, 4):
        a_sh = jnp.where(row >= k, pltpu.roll(a, k, 0), 1.0)
        b_sh = jnp.where(row >= k, pltpu.roll(b, k, 0), 0.0)
        b = a * b_sh + b
        a = a * a_sh
    return a, b


_RNN_PHASES = 4


def _rglru_kernel(x_ref, gate_ref, cw_ref, cb_ref, wrg_ref, brg_ref, wig_ref, big_ref, lam_ref,
                  o_ref, xpad, h_scr, carry):
    ts, nph = _RNN_TS, _RNN_PHASES
    q = ts // nph
    W = RNN_BLOCK_W

    @pl.when(pl.program_id(1) == 0)
    def _():
        xpad[:, 0:_PAD, :] = jnp.zeros((RNN_BLOCKS, _PAD, W), F32)
        carry[...] = jnp.zeros(carry.shape, F32)

    row = lax.broadcasted_iota(jnp.int32, (8, W), 0)

    def slab(c, h_buf):
        cs = pl.ds(pl.multiple_of(c * W, W), W)
        xpad[c, _PAD:_PAD + ts, :] = x_ref[0, :, cs].astype(F32)
        xs = {off: xpad[c, pl.ds(_PAD + off, q, stride=nph), :]
              for off in range(-(CONV_W - 1), nph)}
        w = [cw_ref[k:k + 1, cs] for k in range(CONV_W)]
        parts = []
        for j in range(nph):
            acc = cb_ref[:, cs] + w[0] * xs[j - (CONV_W - 1)]
            for k in range(1, CONV_W):
                acc = acc + w[k] * xs[j - (CONV_W - 1) + k]
            parts.append(acc)
        xc = jnp.concatenate(parts, axis=0)
        xb = xc.astype(BF16)
        r = jax.nn.sigmoid(
            jnp.dot(xb, wrg_ref[c].astype(BF16), preferred_element_type=F32) + brg_ref[:, cs])
        ig = jax.nn.sigmoid(
            jnp.dot(xb, wig_ref[c].astype(BF16), preferred_element_type=F32) + big_ref[:, cs])
        log_a = (-LRU_C * _softplus(-lam_ref[:, cs])) * r
        a = jnp.exp(log_a)
        om = 1.0 - a * a
        b = jnp.where(om > 0.0, om * lax.rsqrt(om), 0.0) * (ig * xc)

        ca = [a[j * q:(j + 1) * q] for j in range(nph)]
        cb = [b[j * q:(j + 1) * q] for j in range(nph)]
        for j in range(1, nph):
            cb[j] = ca[j] * cb[j - 1] + cb[j]
            ca[j] = ca[j] * ca[j - 1]
        h_in = carry[c]
        seg_in = []
        for m in range(q // 8):
            pa, pb = _scan8(ca[-1][m * 8:(m + 1) * 8], cb[-1][m * 8:(m + 1) * 8], row)
            after = pa * h_in + pb
            seg_in.append(jnp.where(row >= 1, pltpu.roll(after, 1, 0), h_in))
            h_in = after[7:8, :]
        carry[c] = h_in
        seg_in = jnp.concatenate(seg_in, axis=0)
        for j in range(nph):
            h_buf[pl.ds(j, q, stride=nph), :] = ca[j] * seg_in + cb[j]
        xpad[c, 0:_PAD, :] = xpad[c, ts:ts + _PAD, :]
        g = gate_ref[0, :, cs].astype(F32)
        o_ref[0, :, cs] = (h_buf[...] * (g * jax.nn.sigmoid(g))).astype(BF16)

    def slab_pair(p, _):
        slab(2 * p, h_scr.at[0])
        slab(2 * p + 1, h_scr.at[1])
        return 0

    lax.fori_loop(0, RNN_BLOCKS // 2, slab_pair, 0)


def _rglru(p_nat, conv_w, conv_b, w_rg, b_rg, w_ig, b_ig, lam, B, S):
    ts, C, W = _RNN_TS, D_RNN, RNN_BLOCK_W
    row_spec = lambda: pl.BlockSpec((1, C), lambda b, t: (0, 0))
    blk_spec = lambda: pl.BlockSpec((RNN_BLOCKS, W, W), lambda b, t: (0, 0, 0))
    return pl.pallas_call(
        _rglru_kernel,
        grid=(B, S // ts),
        in_specs=[pl.BlockSpec((1, ts, C), lambda b, t: (b, t, 0)),
                  pl.BlockSpec((1, ts, C), lambda b, t: (b, t, 1)),
                  pl.BlockSpec((CONV_W, C), lambda b, t: (0, 0)),
                  row_spec(), blk_spec(), row_spec(), blk_spec(), row_spec(), row_spec()],
        out_specs=pl.BlockSpec((1, ts, C), lambda b, t: (b, t, 0)),
        out_shape=jax.ShapeDtypeStruct((B, S, C), BF16),
        scratch_shapes=[pltpu.VMEM((RNN_BLOCKS, ts + _PAD, W), F32), pltpu.VMEM((2, ts, W), F32),
                        pltpu.VMEM((RNN_BLOCKS, 1, W), F32)],
        compiler_params=_cparams(2),
        name="rglru",
    )(p_nat, p_nat, conv_w, conv_b.reshape(1, C), w_rg, b_rg.reshape(1, C), w_ig,
      b_ig.reshape(1, C), lam.reshape(1, C))


_SCALE = HEAD_DIM ** -0.5
_ATTN_BATCH = 8


def _attend(q, k, v, has_prev):
    n, nk, _ = k.shape
    s = jnp.einsum("nqd,nkd->nqk", q, k, preferred_element_type=F32)
    qi = lax.broadcasted_iota(jnp.int32, (1, ATTN_BLOCK, nk), 1)
    kj = lax.broadcasted_iota(jnp.int32, (1, ATTN_BLOCK, nk), 2)
    if has_prev:
        d = kj - qi
        valid = (d >= 0) & (d <= ATTN_BLOCK)
    else:
        valid = kj <= qi
    s = jnp.where(valid, s, MASK_VALUE)
    m = jnp.max(s, axis=-1, keepdims=True)
    p = jnp.exp2((s - m) * (_SCALE * math.log2(math.e)))
    v_ones = jnp.concatenate([v, jnp.ones_like(v)], axis=-1)
    oe = jnp.einsum("nqk,nkd->nqd", p.astype(BF16), v_ones, preferred_element_type=F32)
    l = oe[:, :, HEAD_DIM:]
    o = oe[:, :, :HEAD_DIM] * (1.0 / l)
    return o, m * _SCALE + jnp.log(l)


def _attn_kernel(g0_ref, g1_ref, g2_ref, gate_ref, out_ref, o_scr, ld_scr):
    S = o_scr.shape[1]
    blk = ATTN_BLOCK

    def run_blocks(ref, bases, has_prev):
        res = []
        for i in range(0, len(bases), _ATTN_BATCH):
            chunk = bases[i:i + _ATTN_BATCH]
            lo = blk if has_prev else 0
            q = jnp.stack([ref[0, 0, 0, b:b + blk, :] for b in chunk])
            k = jnp.stack([ref[1, 0, 0, b - lo:b + blk, :] for b in chunk])
            v = jnp.stack([ref[2, 0, 0, b - lo:b + blk, :] for b in chunk])
            o, ld = _attend(q, k, v, has_prev)
            res += [(o[j], ld[j]) for j in range(len(chunk))]
        return res

    for gi, (ref, dil) in enumerate(zip((g0_ref, g1_ref, g2_ref), DILATIONS)):
        L = S // dil
        for has_prev in (False, True):
            blocks = [(r, n) for r in range(dil) for n in range(L // blk) if (n > 0) == has_prev]
            if not blocks:
                continue
            outs = run_blocks(ref, [r * L + n * blk for r, n in blocks], has_prev)
            for (r, n), (o, ld) in zip(blocks, outs):
                if dil == 1:
                    rows = pl.ds(n * blk, blk)
                else:
                    rows = pl.ds(n * blk * dil + r, blk, stride=dil)
                o_scr[gi, rows, :] = o
                ld_scr[gi, rows, :] = ld

    ch = 256

    def merge(c, _):
        rows = pl.ds(pl.multiple_of(c * ch, ch), ch)
        l0, l1, l2 = ld_scr[0, rows, :], ld_scr[1, rows, :], ld_scr[2, rows, :]
        m = jnp.maximum(jnp.maximum(l0, l1), l2)
        e0, e1, e2 = jnp.exp(l0 - m), jnp.exp(l1 - m), jnp.exp(l2 - m)
        num = e0 * o_scr[0, rows, :] + e1 * o_scr[1, rows, :] + e2 * o_scr[2, rows, :]
        o = num * (1.0 / (e0 + e1 + e2))
        g = gate_ref[0, rows, :].astype(F32)
        out_ref[0, rows, :] = (o * (g * jax.nn.sigmoid(g))).astype(BF16)
        return 0

    lax.fori_loop(0, S // ch, merge, 0)


def _attention(qkv, p_nat, B, S):
    H = HEADS_PER_GROUP
    qkv_spec = lambda: pl.BlockSpec((3, 1, 1, S, HEAD_DIM), lambda b, h: (0, b, h, 0, 0))
    return pl.pallas_call(
        _attn_kernel,
        grid=(B, H),
        in_specs=[qkv_spec(), qkv_spec(), qkv_spec(),
                  pl.BlockSpec((1, S, HEAD_DIM),
                               lambda b, h: (b, 0, _P_GATE_BLK * _COL_BLK // HEAD_DIM + h))],
        out_specs=pl.BlockSpec((1, S, HEAD_DIM), lambda b, h: (b, 0, h)),
        out_shape=jax.ShapeDtypeStruct((B, S, H * HEAD_DIM), BF16),
        scratch_shapes=[pltpu.VMEM((3, S, HEAD_DIM), F32), pltpu.VMEM((3, S, HEAD_DIM), F32)],
        compiler_params=_cparams(2),
        name="dilated_attn",
    )(*qkv, p_nat)


def _cast_kernel(w_ref, o_ref):
    o_ref[...] = w_ref[...].astype(BF16)


def _cast_bf16(w):
    K, N = w.shape
    tk = 512
    return pl.pallas_call(
        _cast_kernel,
        grid=(K // tk,),
        in_specs=[pl.BlockSpec((tk, N), lambda i: (i, 0))],
        out_specs=pl.BlockSpec((tk, N), lambda i: (i, 0)),
        out_shape=jax.ShapeDtypeStruct((K, N), BF16),
        compiler_params=_cparams(1),
        name="cast_bf16",
    )(w)


_OUT_TM = 256


def _out_kernel(u_rnn_ref, u_attn_ref, g_rnn_ref, g_attn_ref, x_ref, w_rnn_ref, w_attn_ref,
                w_o_ref, g_post_ref, o_ref):
    y_rnn = jnp.dot(u_rnn_ref[...], w_rnn_ref[...], preferred_element_type=F32)
    y_attn = jnp.dot(u_attn_ref[...], w_attn_ref[...], preferred_element_type=F32)
    merged = (jax.nn.sigmoid(g_rnn_ref[...].astype(F32)) * y_rnn
              + jax.nn.sigmoid(g_attn_ref[...].astype(F32)) * y_attn)
    y = jnp.dot(merged.astype(BF16), w_o_ref[...], preferred_element_type=F32)
    ms = jnp.mean(y * y, axis=-1, keepdims=True)
    o_ref[...] = x_ref[...] + y * lax.rsqrt(ms + NORM_EPS) * g_post_ref[...]


def _output(u_rnn, u_attn, p_nat, x2d, w_rnn_b, w_attn_b, w_o_b, g_post):
    M, D = x2d.shape
    tm = _OUT_TM
    resident = lambda shape: pl.BlockSpec(shape, lambda i: (0, 0), pipeline_mode=pl.Buffered(1))
    return pl.pallas_call(
        _out_kernel,
        grid=(M // tm,),
        in_specs=[pl.BlockSpec((tm, D_RNN), lambda i: (i, 0)),
                  pl.BlockSpec((tm, u_attn.shape[1]), lambda i: (i, 0)),
                  pl.BlockSpec((tm, D), lambda i: (i, _P_GRNN_BLK * _COL_BLK // D)),
                  pl.BlockSpec((tm, D), lambda i: (i, _P_GATTN_BLK * _COL_BLK // D)),
                  pl.BlockSpec((tm, D), lambda i: (i, 0)),
                  resident(w_rnn_b.shape), resident(w_attn_b.shape), resident(w_o_b.shape),
                  pl.BlockSpec((1, D), lambda i: (0, 0))],
        out_specs=pl.BlockSpec((tm, D), lambda i: (i, 0)),
        out_shape=jax.ShapeDtypeStruct((M, D), F32),
        compiler_params=_cparams(1),
        name="output",
    )(u_rnn, u_attn, p_nat, p_nat, x2d, w_rnn_b, w_attn_b, w_o_b, g_post.reshape(1, D))


def _layer(x, ln_pre_g, w_in, conv_w, conv_b, w_rg, b_rg, w_ig, b_ig, lru_lambda,
           w_rnn_out, w_attn_out, w_o, ln_post_g):
    B, S, D = x.shape
    M = B * S
    h0, h4, h16 = _prenorm(x, ln_pre_g)
    p_nat = _proj_nat(h0.reshape(M, D), w_in)
    qkv = [
        _proj_qkv(h.reshape(M, D), w_in, _rope_tables(S, dil), g, B, S)
        for g, (h, dil) in enumerate(zip((h0, h4, h16), DILATIONS))
    ]
    p3 = p_nat.reshape(B, S, p_nat.shape[1])
    u_rnn = _rglru(p3, conv_w, conv_b, w_rg, b_rg, w_ig, b_ig, lru_lambda, B, S)
    u_attn = _attention(qkv, p3, B, S)
    out = _output(u_rnn.reshape(M, D_RNN), u_attn.reshape(M, -1), p_nat, x.reshape(M, D),
                  _cast_bf16(w_rnn_out), _cast_bf16(w_attn_out), _cast_bf16(w_o), ln_post_g)
    return out.reshape(B, S, D)


def kernel(x, ln_pre_g, w_in, conv_w, conv_b, w_rg, b_rg, w_ig, b_ig, lru_lambda,
           w_rnn_out, w_attn_out, w_o, ln_post_g):
    for layer in range(ln_pre_g.shape[0]):
        x = _layer(x, ln_pre_g[layer], w_in[layer], conv_w[layer], conv_b[layer], w_rg[layer],
                   b_rg[layer], w_ig[layer], b_ig[layer], lru_lambda[layer], w_rnn_out[layer],
                   w_attn_out[layer], w_o[layer], ln_post_g[layer])
    return x
```

```python
import functools
import math

import jax
import jax.numpy as jnp
import numpy as np
from jax import lax
from jax.experimental import pallas as pl
from jax.experimental.pallas import tpu as pltpu

D_MODEL = 2048
D_RNN = 2048
RNN_BLOCKS = 16
RNN_BLOCK_W = 128
CONV_W = 4
LRU_C = 8.0
HEAD_DIM = 128
HEADS_PER_GROUP = 8
DILATIONS = (1, 4, 16)
ATTN_BLOCK = 128
ROPE_THETA = 500000.0
ROT_DIM = 32
NORM_EPS = 1e-6
MASK_VALUE = -1e30

BF16 = jnp.bfloat16
F32 = jnp.float32

_COL_BLK = 1024
_Q_BLK0 = 4
_NAT_BLOCKS = 9
_P_GRNN_BLK = 4
_P_GATTN_BLK = 6
_P_GATE_BLK = 8


def _nat_w_block(j):
    return jnp.where(j < 4, j, jnp.where(j < 8, j + 10, 13))

_HEADS_PER_DOT = 2

_VMEM_LIMIT = 56 * 1024 * 1024


def _cparams(n_axes):
    return pltpu.CompilerParams(
        dimension_semantics=("arbitrary",) * n_axes, vmem_limit_bytes=_VMEM_LIMIT)


_NORM_T = 256


def _prenorm_kernel(x_ref, g_ref, h0_ref, h4_ref, h16_ref, y_scr):
    x = x_ref[0]
    ms = jnp.mean(x * x, axis=-1, keepdims=True)
    y = x * lax.rsqrt(ms + NORM_EPS) * g_ref[...]
    h0_ref[0] = y.astype(BF16)
    for c in range(y_scr.shape[0]):
        lanes = slice(c * 128, (c + 1) * 128)
        y_scr[c] = y[:, lanes]
        for r in range(4):
            h4_ref[0, r, :, lanes] = y_scr[c, pl.ds(r, _NORM_T // 4, stride=4), :].astype(BF16)
        for r in range(16):
            h16_ref[0, r, :, lanes] = y_scr[c, pl.ds(r, _NORM_T // 16, stride=16), :].astype(BF16)


def _prenorm(x, g):
    B, S, D = x.shape
    T = _NORM_T
    return pl.pallas_call(
        _prenorm_kernel,
        grid=(B, S // T),
        in_specs=[pl.BlockSpec((1, T, D), lambda b, t: (b, t, 0)),
                  pl.BlockSpec((1, D), lambda b, t: (0, 0))],
        out_specs=[pl.BlockSpec((1, T, D), lambda b, t: (b, t, 0)),
                   pl.BlockSpec((1, 4, T // 4, D), lambda b, t: (b, 0, t, 0)),
                   pl.BlockSpec((1, 16, T // 16, D), lambda b, t: (b, 0, t, 0))],
        out_shape=[jax.ShapeDtypeStruct((B, S, D), BF16),
                   jax.ShapeDtypeStruct((B, 4, S // 4, D), BF16),
                   jax.ShapeDtypeStruct((B, 16, S // 16, D), BF16)],
        scratch_shapes=[pltpu.VMEM((D // 128, T, 128), F32)],
        compiler_params=_cparams(2),
        name="prenorm",
    )(x, g.reshape(1, D))


def _cast_weight_once(w_ref, wb_ref):
    @pl.when(pl.program_id(1) == 0)
    def _():
        wb_ref[...] = w_ref[...].astype(BF16)


def _proj_nat_kernel(lhs_ref, w_ref, o_ref, wb_ref):
    _cast_weight_once(w_ref, wb_ref)
    o_ref[...] = jnp.dot(lhs_ref[...], wb_ref[...], preferred_element_type=F32).astype(BF16)


def _proj_nat(h, w_in):
    M, K = h.shape
    tm, tn = 1024, _COL_BLK
    return pl.pallas_call(
        _proj_nat_kernel,
        grid=(_NAT_BLOCKS, M // tm),
        in_specs=[pl.BlockSpec((tm, K), lambda j, i: (i, 0)),
                  pl.BlockSpec((K, tn), lambda j, i: (0, _nat_w_block(j)))],
        out_specs=pl.BlockSpec((tm, tn), lambda j, i: (i, j)),
        out_shape=jax.ShapeDtypeStruct((M, _NAT_BLOCKS * tn), BF16),
        scratch_shapes=[pltpu.VMEM((K, tn), BF16)],
        compiler_params=_cparams(2),
        name="proj_nat",
    )(h, w_in)


def _proj_qkv_kernel(lhs_ref, w_ref, cos_ref, sa_ref, sb_ref, o_ref, wb_ref, acc_scr, *,
                     row_tiles, n_tiles):
    t = pl.program_id(0)

    @pl.when(t == 0)
    def _():
        acc_scr[...] = jnp.zeros(acc_scr.shape, F32)

    @pl.when((t % row_tiles == 0) & (t < n_tiles))
    def _():
        wb_ref[...] = w_ref[...].astype(BF16)

    def body(acc_cur, acc_prev):
        is_qk = (t - 1) // row_tiles < 2
        cos = jnp.where(is_qk, cos_ref[...], 1.0)
        sa = jnp.where(is_qk, sa_ref[...], 0.0)
        sb = jnp.where(is_qk, sb_ref[...], 0.0)
        for c0 in range(0, HEADS_PER_GROUP, _HEADS_PER_DOT):
            lanes = slice(c0 * HEAD_DIM, (c0 + _HEADS_PER_DOT) * HEAD_DIM)
            acc_cur[:, lanes] = jnp.dot(lhs_ref[...], wb_ref[:, lanes],
                                        preferred_element_type=F32)
            for c in range(c0, c0 + _HEADS_PER_DOT):
                v = acc_prev[:, c * HEAD_DIM:(c + 1) * HEAD_DIM]
                rot = (v * cos + pltpu.roll(v, HEAD_DIM - ROT_DIM // 2, 1) * sa
                       + pltpu.roll(v, ROT_DIM // 2, 1) * sb)
                o_ref[0, 0, c] = rot.astype(BF16)

    for parity in range(2):
        pl.when(t % 2 == parity)(
            functools.partial(body, acc_scr.at[parity], acc_scr.at[1 - parity]))


def _proj_qkv(h_perm, w_in, tables, group, B, S):
    M, K = h_perm.shape
    tm, tn = 1024, _COL_BLK
    spt = S // tm
    row_tiles = M // tm
    n_tiles = 3 * row_tiles
    mm = lambda t: jnp.minimum(t, n_tiles - 1)
    ep = lambda t: jnp.maximum(t - 1, 0)
    tbl_spec = pl.BlockSpec((tm, HEAD_DIM), lambda t: (ep(t) % row_tiles % spt, 0))
    return pl.pallas_call(
        functools.partial(_proj_qkv_kernel, row_tiles=row_tiles, n_tiles=n_tiles),
        grid=(n_tiles + 1,),
        in_specs=[pl.BlockSpec((tm, K), lambda t: (mm(t) % row_tiles, 0)),
                  pl.BlockSpec((K, tn),
                               lambda t: (0, _Q_BLK0 + group + 3 * (mm(t) // row_tiles))),
                  tbl_spec, tbl_spec, tbl_spec],
        out_specs=pl.BlockSpec(
            (1, 1, HEADS_PER_GROUP, tm, HEAD_DIM),
            lambda t: (ep(t) // row_tiles, ep(t) % row_tiles // spt, 0, ep(t) % row_tiles % spt, 0)),
        out_shape=jax.ShapeDtypeStruct((3, B, HEADS_PER_GROUP, S, HEAD_DIM), BF16),
        scratch_shapes=[pltpu.VMEM((K, tn), BF16), pltpu.VMEM((2, tm, tn), F32)],
        compiler_params=_cparams(1),
        name=f"proj_qkv_g{group}",
    )(h_perm, w_in, *tables)


def _rope_tables(S, dilation):
    half = ROT_DIM // 2
    inv = np.exp(-math.log(ROPE_THETA) * np.arange(half, dtype=np.float32)
                 * np.float32(2.0 / ROT_DIM)).astype(np.float32)
    L = S // dilation
    p = np.arange(S)
    pos = (p % L) * dilation + p // L
    ang = pos.astype(np.float32)[:, None] * inv[None, :]
    cos, sin = np.cos(ang).astype(np.float32), np.sin(ang).astype(np.float32)
    cos_t = np.ones((S, HEAD_DIM), np.float32)
    sin_a = np.zeros((S, HEAD_DIM), np.float32)
    sin_b = np.zeros((S, HEAD_DIM), np.float32)
    cos_t[:, :half] = cos
    cos_t[:, half:ROT_DIM] = cos
    sin_a[:, :half] = -sin
    sin_b[:, half:ROT_DIM] = sin
    return jnp.asarray(cos_t), jnp.asarray(sin_a), jnp.asarray(sin_b)


_RNN_TS = 256
_PAD = 8


def _softplus(z):
    return jnp.maximum(z, 0.0) + jnp.log1p(jnp.exp(-jnp.abs(z)))


def _scan8(a, b, row):
    for k in (1, 2, 4):
        a_sh = jnp.where(row >= k, pltpu.roll(a, k, 0), 1.0)
        b_sh = jnp.where(row >= k, pltpu.roll(b, k, 0), 0.0)
        b = a * b_sh + b
        a = a * a_sh
    return a, b


_RNN_PHASES = 4


def _rglru_gates(c, x_ref, cw_ref, cb_ref, wrg_ref, wig_ref, xpad, stage):
    ts, nph = _RNN_TS, _RNN_PHASES
    q = ts // nph
    W = RNN_BLOCK_W
    cs = slice(c * W, (c + 1) * W)
    xpad[c, _PAD:_PAD + ts, :] = x_ref[:, cs].astype(F32)
    xs = {off: xpad[c, pl.ds(_PAD + off, q, stride=nph), :]
          for off in range(-(CONV_W - 1), nph)}
    w = [cw_ref[k:k + 1, cs] for k in range(CONV_W)]
    parts = []
    for j in range(nph):
        acc = cb_ref[:, cs] + w[0] * xs[j - (CONV_W - 1)]
        for k in range(1, CONV_W):
            acc = acc + w[k] * xs[j - (CONV_W - 1) + k]
        parts.append(acc)
    xc = jnp.concatenate(parts, axis=0)
    xb = xc.astype(BF16)
    xpad[c, 0:_PAD, :] = xpad[c, ts:ts + _PAD, :]
    stage[0] = xc
    stage[1] = jnp.dot(xb, wrg_ref[c].astype(BF16), preferred_element_type=F32)
    stage[2] = jnp.dot(xb, wig_ref[c].astype(BF16), preferred_element_type=F32)


def _rglru_scan(c, stage, gate_ref, brg_ref, big_ref, lam_ref, o_ref, h_buf, carry):
    ts, nph = _RNN_TS, _RNN_PHASES
    q = ts // nph
    W = RNN_BLOCK_W
    row = lax.broadcasted_iota(jnp.int32, (8, W), 0)
    cs = slice(c * W, (c + 1) * W)
    xc = stage[0]
    r = jax.nn.sigmoid(stage[1] + brg_ref[:, cs])
    ig = jax.nn.sigmoid(stage[2] + big_ref[:, cs])
    log_a = (-LRU_C * _softplus(-lam_ref[:, cs])) * r
    a = jnp.exp(log_a)
    om = 1.0 - a * a
    stage[1] = a
    stage[2] = jnp.where(om > 0.0, om * lax.rsqrt(om), 0.0) * (ig * xc)
    part = lambda k, j: stage[k, j * q:(j + 1) * q, :]

    tot_a, tot_b = part(1, 0), part(2, 0)
    for j in range(1, nph):
        tot_b = part(1, j) * tot_b + part(2, j)
        tot_a = part(1, j) * tot_a
    h_in = carry[c]
    seg_in = []
    for m in range(q // 8):
        pa, pb = _scan8(tot_a[m * 8:(m + 1) * 8], tot_b[m * 8:(m + 1) * 8], row)
        after = pa * h_in + pb
        seg_in.append(jnp.where(row >= 1, pltpu.roll(after, 1, 0), h_in))
        h_in = after[7:8, :]
    carry[c] = h_in
    h = jnp.concatenate(seg_in, axis=0)
    for j in range(nph):
        h = part(1, j) * h + part(2, j)
        h_buf[pl.ds(j, q, stride=nph), :] = h
    g = gate_ref[:, cs].astype(F32)
    o_ref[:, cs] = (h_buf[...] * (g * jax.nn.sigmoid(g))).astype(BF16)


_SCALE = HEAD_DIM ** -0.5
_ATTN_BATCH = 8


def _attend(q, k, v, has_prev):
    n, nk, _ = k.shape
    s = jnp.einsum("nqd,nkd->nqk", q, k, preferred_element_type=F32)
    qi = lax.broadcasted_iota(jnp.int32, (1, ATTN_BLOCK, nk), 1)
    kj = lax.broadcasted_iota(jnp.int32, (1, ATTN_BLOCK, nk), 2)
    if has_prev:
        d = kj - qi
        valid = (d >= 0) & (d <= ATTN_BLOCK)
    else:
        valid = kj <= qi
    s = jnp.where(valid, s, MASK_VALUE)
    m = jnp.max(s, axis=-1, keepdims=True)
    p = jnp.exp2((s - m) * (_SCALE * math.log2(math.e)))
    v_ones = jnp.concatenate([v, jnp.ones_like(v)], axis=-1)
    oe = jnp.einsum("nqk,nkd->nqd", p.astype(BF16), v_ones, preferred_element_type=F32)
    l = oe[:, :, HEAD_DIM:]
    o = oe[:, :, :HEAD_DIM] * (1.0 / l)
    return o, m * _SCALE + jnp.log(l)


def _attn_kernel(g0_ref, g1_ref, g2_ref, gate_ref, out_ref, o_scr, ld_scr):
    S = o_scr.shape[1]
    blk = ATTN_BLOCK

    def run_blocks(ref, bases, has_prev):
        res = []
        for i in range(0, len(bases), _ATTN_BATCH):
            chunk = bases[i:i + _ATTN_BATCH]
            lo = blk if has_prev else 0
            q = jnp.stack([ref[0, 0, 0, b:b + blk, :] for b in chunk])
            k = jnp.stack([ref[1, 0, 0, b - lo:b + blk, :] for b in chunk])
            v = jnp.stack([ref[2, 0, 0, b - lo:b + blk, :] for b in chunk])
            o, ld = _attend(q, k, v, has_prev)
            res += [(o[j], ld[j]) for j in range(len(chunk))]
        return res

    for gi, (ref, dil) in enumerate(zip((g0_ref, g1_ref, g2_ref), DILATIONS)):
        L = S // dil
        for has_prev in (False, True):
            blocks = [(r, n) for r in range(dil) for n in range(L // blk) if (n > 0) == has_prev]
            if not blocks:
                continue
            outs = run_blocks(ref, [r * L + n * blk for r, n in blocks], has_prev)
            for (r, n), (o, ld) in zip(blocks, outs):
                if dil == 1:
                    rows = pl.ds(n * blk, blk)
                else:
                    rows = pl.ds(n * blk * dil + r, blk, stride=dil)
                o_scr[gi, rows, :] = o
                ld_scr[gi, rows, :] = ld

    ch = 256

    def merge(c, _):
        rows = pl.ds(pl.multiple_of(c * ch, ch), ch)
        l0, l1, l2 = ld_scr[0, rows, :], ld_scr[1, rows, :], ld_scr[2, rows, :]
        m = jnp.maximum(jnp.maximum(l0, l1), l2)
        e0, e1, e2 = jnp.exp(l0 - m), jnp.exp(l1 - m), jnp.exp(l2 - m)
        num = e0 * o_scr[0, rows, :] + e1 * o_scr[1, rows, :] + e2 * o_scr[2, rows, :]
        o = num * (1.0 / (e0 + e1 + e2))
        g = gate_ref[0, rows, :].astype(F32)
        out_ref[0, rows, :] = (o * (g * jax.nn.sigmoid(g))).astype(BF16)
        return 0

    lax.fori_loop(0, S // ch, merge, 0)


def _attention(qkv, p_nat, B, S):
    H = HEADS_PER_GROUP
    qkv_spec = lambda: pl.BlockSpec((3, 1, 1, S, HEAD_DIM), lambda b, h: (0, b, h, 0, 0))
    return pl.pallas_call(
        _attn_kernel,
        grid=(B, H),
        in_specs=[qkv_spec(), qkv_spec(), qkv_spec(),
                  pl.BlockSpec((1, S, HEAD_DIM),
                               lambda b, h: (b, 0, _P_GATE_BLK * _COL_BLK // HEAD_DIM + h))],
        out_specs=pl.BlockSpec((1, S, HEAD_DIM), lambda b, h: (b, 0, h)),
        out_shape=jax.ShapeDtypeStruct((B, S, H * HEAD_DIM), BF16),
        scratch_shapes=[pltpu.VMEM((3, S, HEAD_DIM), F32), pltpu.VMEM((3, S, HEAD_DIM), F32)],
        compiler_params=_cparams(2),
        name="dilated_attn",
    )(*qkv, p_nat)


def _cast_kernel(w_ref, o_ref):
    o_ref[...] = w_ref[...].astype(BF16)


def _cast_bf16(w):
    K, N = w.shape
    tk = 512
    return pl.pallas_call(
        _cast_kernel,
        grid=(K // tk,),
        in_specs=[pl.BlockSpec((tk, N), lambda i: (i, 0))],
        out_specs=pl.BlockSpec((tk, N), lambda i: (i, 0)),
        out_shape=jax.ShapeDtypeStruct((K, N), BF16),
        compiler_params=_cparams(1),
        name="cast_bf16",
    )(w)


_DOT_LANES = 256


def _tail_kernel(xr_ref, gr_ref, cw_ref, cb_ref, wrg_ref, brg_ref, wig_ref, big_ref, lam_ref,
                 u_attn_ref, g_rnn_ref, g_attn_ref, x_ref, w_rnn_ref, w_attn_ref, w_o_ref,
                 g_post_ref, o_ref, xpad, stage, h_scr, carry, u_scr, merged_scr, y_scr, *,
                 tiles_per_seq):
    s = pl.program_id(0)

    @pl.when(s == 0)
    def _():
        u_scr[...] = jnp.zeros(u_scr.shape, BF16)

    @pl.when(s % tiles_per_seq == 0)
    def _():
        xpad[:, 0:_PAD, :] = jnp.zeros((RNN_BLOCKS, _PAD, RNN_BLOCK_W), F32)
        carry[...] = jnp.zeros(carry.shape, F32)

    def body(u_next, u_cur):
        n_chunks = D_MODEL // _DOT_LANES
        gates = lambda c: _rglru_gates(c, xr_ref, cw_ref, cb_ref, wrg_ref, wig_ref, xpad,
                                       stage.at[c % 2])
        gates(0)
        for c in range(RNN_BLOCKS):
            if c + 1 < RNN_BLOCKS:
                gates(c + 1)
            _rglru_scan(c, stage.at[c % 2], gr_ref, brg_ref, big_ref, lam_ref, u_next,
                        h_scr.at[c % 2], carry)
            lanes = slice((c % n_chunks) * _DOT_LANES, (c % n_chunks + 1) * _DOT_LANES)
            if c < n_chunks:
                y_rnn = jnp.dot(u_cur[...], w_rnn_ref[:, lanes], preferred_element_type=F32)
                y_attn = jnp.dot(u_attn_ref[...], w_attn_ref[:, lanes],
                                 preferred_element_type=F32)
                merged_scr[:, lanes] = (
                    jax.nn.sigmoid(g_rnn_ref[:, lanes].astype(F32)) * y_rnn
                    + jax.nn.sigmoid(g_attn_ref[:, lanes].astype(F32)) * y_attn).astype(BF16)
            else:
                y_scr[:, lanes] = jnp.dot(merged_scr[...], w_o_ref[:, lanes],
                                          preferred_element_type=F32)
        y = y_scr[...]
        ms = jnp.mean(y * y, axis=-1, keepdims=True)
        o_ref[...] = x_ref[...] + y * lax.rsqrt(ms + NORM_EPS) * g_post_ref[...]

    for parity in range(2):
        pl.when(s % 2 == parity)(
            functools.partial(body, u_scr.at[parity], u_scr.at[1 - parity]))


def _tail(p_nat, u_attn, x2d, conv_w, conv_b, w_rg, b_rg, w_ig, b_ig, lam,
          w_rnn_b, w_attn_b, w_o_b, g_post, S):
    M, D = x2d.shape
    tm, C, W = _RNN_TS, D_RNN, RNN_BLOCK_W
    n_tiles = M // tm
    assert RNN_BLOCKS == 2 * (D // _DOT_LANES)
    nxt = lambda s: jnp.minimum(s, n_tiles - 1)
    cur = lambda s: jnp.maximum(s - 1, 0)
    const = lambda shape: pl.BlockSpec(shape, lambda s: (0,) * len(shape))
    resident = lambda shape: pl.BlockSpec(shape, lambda s: (0, 0), pipeline_mode=pl.Buffered(1))
    return pl.pallas_call(
        functools.partial(_tail_kernel, tiles_per_seq=S // tm),
        grid=(n_tiles + 1,),
        in_specs=[pl.BlockSpec((tm, C), lambda s: (nxt(s), 0)),
                  pl.BlockSpec((tm, C), lambda s: (nxt(s), 1)),
                  const((CONV_W, C)), const((1, C)), const((RNN_BLOCKS, W, W)), const((1, C)),
                  const((RNN_BLOCKS, W, W)), const((1, C)), const((1, C)),
                  pl.BlockSpec((tm, u_attn.shape[1]), lambda s: (cur(s), 0)),
                  pl.BlockSpec((tm, D), lambda s: (cur(s), _P_GRNN_BLK * _COL_BLK // D)),
                  pl.BlockSpec((tm, D), lambda s: (cur(s), _P_GATTN_BLK * _COL_BLK // D)),
                  pl.BlockSpec((tm, D), lambda s: (cur(s), 0)),
                  resident(w_rnn_b.shape), resident(w_attn_b.shape), resident(w_o_b.shape),
                  const((1, D))],
        out_specs=pl.BlockSpec((tm, D), lambda s: (cur(s), 0)),
        out_shape=jax.ShapeDtypeStruct((M, D), F32),
        scratch_shapes=[pltpu.VMEM((RNN_BLOCKS, tm + _PAD, W), F32),
                        pltpu.VMEM((2, 3, tm, W), F32),
                        pltpu.VMEM((2, tm, W), F32),
                        pltpu.VMEM((RNN_BLOCKS, 1, W), F32),
                        pltpu.VMEM((2, tm, C), BF16),
                        pltpu.VMEM((tm, D), BF16),
                        pltpu.VMEM((tm, D), F32)],
        compiler_params=_cparams(1),
        name="rglru_output",
    )(p_nat, p_nat, conv_w, conv_b.reshape(1, C), w_rg, b_rg.reshape(1, C), w_ig,
      b_ig.reshape(1, C), lam.reshape(1, C), u_attn, p_nat, p_nat, x2d,
      w_rnn_b, w_attn_b, w_o_b, g_post.reshape(1, D))


def _layer(x, ln_pre_g, w_in, conv_w, conv_b, w_rg, b_rg, w_ig, b_ig, lru_lambda,
           w_rnn_out, w_attn_out, w_o, ln_post_g):
    B, S, D = x.shape
    M = B * S
    h0, h4, h16 = _prenorm(x, ln_pre_g)
    p_nat = _proj_nat(h0.reshape(M, D), w_in)
    qkv = [
        _proj_qkv(h.reshape(M, D), w_in, _rope_tables(S, dil), g, B, S)
        for g, (h, dil) in enumerate(zip((h0, h4, h16), DILATIONS))
    ]
    p3 = p_nat.reshape(B, S, p_nat.shape[1])
    u_attn = _attention(qkv, p3, B, S)
    out = _tail(p_nat, u_attn.reshape(M, -1), x.reshape(M, D), conv_w, conv_b, w_rg, b_rg, w_ig,
                b_ig, lru_lambda, _cast_bf16(w_rnn_out), _cast_bf16(w_attn_out), _cast_bf16(w_o),
                ln_post_g, S)
    return out.reshape(B, S, D)


def kernel(x, ln_pre_g, w_in, conv_w, conv_b, w_rg, b_rg, w_ig, b_ig, lru_lambda,
           w_rnn_out, w_attn_out, w_o, ln_post_g):
    for layer in range(ln_pre_g.shape[0]):
        x = _layer(x, ln_pre_g[layer], w_in[layer], conv_w[layer], conv_b[layer], w_rg[layer],
                   b_rg[layer], w_ig[layer], b_ig[layer], lru_lambda[layer], w_rnn_out[layer],
                   w_attn_out[layer], w_o[layer], ln_post_g[layer])
    return x
```

```python
import functools
import math

import jax
import jax.numpy as jnp
import numpy as np
from jax import lax
from jax.experimental import pallas as pl
from jax.experimental.pallas import tpu as pltpu

D_MODEL = 2048
D_RNN = 2048
RNN_BLOCKS = 16
RNN_BLOCK_W = 128
CONV_W = 4
LRU_C = 8.0
HEAD_DIM = 128
HEADS_PER_GROUP = 8
DILATIONS = (1, 4, 16)
ATTN_BLOCK = 128
ROPE_THETA = 500000.0
ROT_DIM = 32
NORM_EPS = 1e-6
MASK_VALUE = -1e30

BF16 = jnp.bfloat16
F32 = jnp.float32

_COL_BLK = 1024
_Q_BLK0 = 4
_NAT_BLOCKS = 9
_P_GRNN_BLK = 4
_P_GATTN_BLK = 6
_P_GATE_BLK = 8


def _nat_w_block(j):
    return jnp.where(j < 4, j, jnp.where(j < 8, j + 10, 13))

_HEADS_PER_DOT = 2

_VMEM_LIMIT = 56 * 1024 * 1024


def _cparams(n_axes):
    return pltpu.CompilerParams(
        dimension_semantics=("arbitrary",) * n_axes, vmem_limit_bytes=_VMEM_LIMIT)


_NORM_T = 512


def _prenorm_kernel(x_ref, g_ref, h_ref):
    x = x_ref[...]
    ms = jnp.mean(x * x, axis=-1, keepdims=True)
    h_ref[...] = (x * lax.rsqrt(ms + NORM_EPS) * g_ref[...]).astype(BF16)


def _prenorm(x2d, g):
    M, D = x2d.shape
    T = _NORM_T
    return pl.pallas_call(
        _prenorm_kernel,
        grid=(M // T,),
        in_specs=[pl.BlockSpec((T, D), lambda i: (i, 0)),
                  pl.BlockSpec((1, D), lambda i: (0, 0))],
        out_specs=pl.BlockSpec((T, D), lambda i: (i, 0)),
        out_shape=jax.ShapeDtypeStruct((M, D), BF16),
        compiler_params=_cparams(1),
        name="prenorm",
    )(x2d, g.reshape(1, D))


def _cast_weight_once(w_ref, wb_ref):
    @pl.when(pl.program_id(1) == 0)
    def _():
        wb_ref[...] = w_ref[...].astype(BF16)


_CAST_ROWS = 256


def _proj_nat_kernel(lhs_ref, w_ref, *rest, cast_steps):
    n = len(cast_steps)
    wf_refs, o_ref, wo_refs, wb_ref = rest[:n], rest[n], rest[n + 1:2 * n + 1], rest[2 * n + 1]
    _cast_weight_once(w_ref, wb_ref)
    step = pl.program_id(0) * pl.num_programs(1) + pl.program_id(1)
    for wf_ref, wo_ref, steps in zip(wf_refs, wo_refs, cast_steps):
        @pl.when(step < steps)
        def _(wf_ref=wf_ref, wo_ref=wo_ref):
            wo_ref[...] = wf_ref[...].astype(BF16)
    o_ref[...] = jnp.dot(lhs_ref[...], wb_ref[...], preferred_element_type=F32).astype(BF16)


def _proj_nat(h, w_in, out_weights):
    M, K = h.shape
    tm, tn = 1024, _COL_BLK
    row_tiles = M // tm
    cast_steps = tuple(w.shape[0] // _CAST_ROWS for w in out_weights)
    assert max(cast_steps) <= _NAT_BLOCKS * row_tiles

    def chunk_spec(w, steps):
        return pl.BlockSpec((_CAST_ROWS, w.shape[1]),
                            lambda j, i: (jnp.minimum(j * row_tiles + i, steps - 1), 0))

    return pl.pallas_call(
        functools.partial(_proj_nat_kernel, cast_steps=cast_steps),
        grid=(_NAT_BLOCKS, row_tiles),
        in_specs=[pl.BlockSpec((tm, K), lambda j, i: (i, 0)),
                  pl.BlockSpec((K, tn), lambda j, i: (0, _nat_w_block(j)))]
        + [chunk_spec(w, n) for w, n in zip(out_weights, cast_steps)],
        out_specs=[pl.BlockSpec((tm, tn), lambda j, i: (i, j))]
        + [chunk_spec(w, n) for w, n in zip(out_weights, cast_steps)],
        out_shape=[jax.ShapeDtypeStruct((M, _NAT_BLOCKS * tn), BF16)]
        + [jax.ShapeDtypeStruct(w.shape, BF16) for w in out_weights],
        scratch_shapes=[pltpu.VMEM((K, tn), BF16)],
        compiler_params=_cparams(2),
        name="proj_nat",
    )(h, w_in, *out_weights)


def _proj_qkv_kernel(lhs_ref, w_ref, cos_ref, sa_ref, sb_ref, o_ref, wb_ref, acc_scr, *,
                     row_tiles, n_tiles, dil):
    t = pl.program_id(0)
    rows = acc_scr.shape[2] // dil

    @pl.when(t == 0)
    def _():
        acc_scr[...] = jnp.zeros(acc_scr.shape, F32)

    @pl.when((t % row_tiles == 0) & (t < n_tiles))
    def _():
        wb_ref[...] = w_ref[...].astype(BF16)

    def body(acc_cur, acc_prev):
        is_qk = (t - 1) // row_tiles < 2
        cos = jnp.where(is_qk, cos_ref[...], 1.0)
        sa = jnp.where(is_qk, sa_ref[...], 0.0)
        sb = jnp.where(is_qk, sb_ref[...], 0.0)
        for c0 in range(0, HEADS_PER_GROUP, _HEADS_PER_DOT):
            lanes = slice(c0 * HEAD_DIM, (c0 + _HEADS_PER_DOT) * HEAD_DIM)
            acc = jnp.dot(lhs_ref[...], wb_ref[:, lanes], preferred_element_type=F32)
            for c in range(c0, c0 + _HEADS_PER_DOT):
                acc_cur[c] = acc[:, (c - c0) * HEAD_DIM:(c - c0 + 1) * HEAD_DIM]
            for c in range(c0, c0 + _HEADS_PER_DOT):
                for r in range(dil):
                    v = acc_prev[c, pl.ds(r, rows, stride=dil), :] if dil > 1 else acc_prev[c]
                    rot = (v * cos[r] + pltpu.roll(v, HEAD_DIM - ROT_DIM // 2, 1) * sa[r]
                           + pltpu.roll(v, ROT_DIM // 2, 1) * sb[r])
                    o_ref[0, 0, c, r] = rot.astype(BF16)

    for parity in range(2):
        pl.when(t % 2 == parity)(
            functools.partial(body, acc_scr.at[parity], acc_scr.at[1 - parity]))


def _proj_qkv(h, w_in, tables, group, B, S):
    M, K = h.shape
    dil = DILATIONS[group]
    L = S // dil
    tm, tn = 1024, _COL_BLK
    spt = S // tm
    row_tiles = M // tm
    n_tiles = 3 * row_tiles
    mm = lambda t: jnp.minimum(t, n_tiles - 1)
    ep = lambda t: jnp.maximum(t - 1, 0)
    tbl_spec = pl.BlockSpec((dil, tm // dil, HEAD_DIM), lambda t: (0, ep(t) % row_tiles % spt, 0))
    out = pl.pallas_call(
        functools.partial(_proj_qkv_kernel, row_tiles=row_tiles, n_tiles=n_tiles, dil=dil),
        grid=(n_tiles + 1,),
        in_specs=[pl.BlockSpec((tm, K), lambda t: (mm(t) % row_tiles, 0)),
                  pl.BlockSpec((K, tn),
                               lambda t: (0, _Q_BLK0 + group + 3 * (mm(t) // row_tiles))),
                  tbl_spec, tbl_spec, tbl_spec],
        out_specs=pl.BlockSpec(
            (1, 1, HEADS_PER_GROUP, dil, tm // dil, HEAD_DIM),
            lambda t: (ep(t) // row_tiles, ep(t) % row_tiles // spt, 0, 0,
                       ep(t) % row_tiles % spt, 0)),
        out_shape=jax.ShapeDtypeStruct((3, B, HEADS_PER_GROUP, dil, L, HEAD_DIM), BF16),
        scratch_shapes=[pltpu.VMEM((K, tn), BF16),
                        pltpu.VMEM((2, HEADS_PER_GROUP, tm, HEAD_DIM), F32)],
        compiler_params=_cparams(1),
        name=f"proj_qkv_g{group}",
    )(h, w_in, *[tbl.reshape(dil, L, HEAD_DIM) for tbl in tables])
    return out.reshape(3, B, HEADS_PER_GROUP, S, HEAD_DIM)


def _rope_tables(S, dilation):
    half = ROT_DIM // 2
    inv = np.exp(-math.log(ROPE_THETA) * np.arange(half, dtype=np.float32)
                 * np.float32(2.0 / ROT_DIM)).astype(np.float32)
    L = S // dilation
    p = np.arange(S)
    pos = (p % L) * dilation + p // L
    ang = pos.astype(np.float32)[:, None] * inv[None, :]
    cos, sin = np.cos(ang).astype(np.float32), np.sin(ang).astype(np.float32)
    cos_t = np.ones((S, HEAD_DIM), np.float32)
    sin_a = np.zeros((S, HEAD_DIM), np.float32)
    sin_b = np.zeros((S, HEAD_DIM), np.float32)
    cos_t[:, :half] = cos
    cos_t[:, half:ROT_DIM] = cos
    sin_a[:, :half] = -sin
    sin_b[:, half:ROT_DIM] = sin
    return jnp.asarray(cos_t), jnp.asarray(sin_a), jnp.asarray(sin_b)


_RNN_TS = 256
_PAD = 8


def _softplus(z):
    return jnp.maximum(z, 0.0) + jnp.log1p(jnp.exp(-jnp.abs(z)))


def _scan8(a, b, row):
    for k in (1, 2, 4):
        a_sh = jnp.where(row >= k, pltpu.roll(a, k, 0), 1.0)
        b_sh = jnp.where(row >= k, pltpu.roll(b, k, 0), 0.0)
        b = a * b_sh + b
        a = a * a_sh
    return a, b


_RNN_PHASES = 4


def _rglru_gates(c, x_ref, cw_ref, cb_ref, wrg_ref, wig_ref, xpad, stage):
    ts, nph = _RNN_TS, _RNN_PHASES
    q = ts // nph
    W = RNN_BLOCK_W
    cs = slice(c * W, (c + 1) * W)
    xpad[c, _PAD:_PAD + ts, :] = x_ref[:, cs].astype(F32)
    xs = {off: xpad[c, pl.ds(_PAD + off, q, stride=nph), :]
          for off in range(-(CONV_W - 1), nph)}
    w = [cw_ref[k:k + 1, cs] for k in range(CONV_W)]
    parts = []
    for j in range(nph):
        acc = cb_ref[:, cs] + w[0] * xs[j - (CONV_W - 1)]
        for k in range(1, CONV_W):
            acc = acc + w[k] * xs[j - (CONV_W - 1) + k]
        parts.append(acc)
    xc = jnp.concatenate(parts, axis=0)
    xb = xc.astype(BF16)
    xpad[c, 0:_PAD, :] = xpad[c, ts:ts + _PAD, :]
    stage[0] = xc
    stage[1] = jnp.dot(xb, wrg_ref[c].astype(BF16), preferred_element_type=F32)
    stage[2] = jnp.dot(xb, wig_ref[c].astype(BF16), preferred_element_type=F32)


def _rglru_scan(c, stage, gate_ref, brg_ref, big_ref, lam_ref, o_ref, h_buf, carry):
    ts, nph = _RNN_TS, _RNN_PHASES
    q = ts // nph
    W = RNN_BLOCK_W
    row = lax.broadcasted_iota(jnp.int32, (8, W), 0)
    cs = slice(c * W, (c + 1) * W)
    xc = stage[0]
    r = jax.nn.sigmoid(stage[1] + brg_ref[:, cs])
    ig = jax.nn.sigmoid(stage[2] + big_ref[:, cs])
    log_a = (-LRU_C * _softplus(-lam_ref[:, cs])) * r
    a = jnp.exp(log_a)
    om = 1.0 - a * a
    stage[1] = a
    stage[2] = jnp.where(om > 0.0, om * lax.rsqrt(om), 0.0) * (ig * xc)
    part = lambda k, j: stage[k, j * q:(j + 1) * q, :]

    tot_a, tot_b = part(1, 0), part(2, 0)
    for j in range(1, nph):
        tot_b = part(1, j) * tot_b + part(2, j)
        tot_a = part(1, j) * tot_a
    h_in = carry[c]
    seg_in = []
    for m in range(q // 8):
        pa, pb = _scan8(tot_a[m * 8:(m + 1) * 8], tot_b[m * 8:(m + 1) * 8], row)
        after = pa * h_in + pb
        seg_in.append(jnp.where(row >= 1, pltpu.roll(after, 1, 0), h_in))
        h_in = after[7:8, :]
    carry[c] = h_in
    h = jnp.concatenate(seg_in, axis=0)
    for j in range(nph):
        h = part(1, j) * h + part(2, j)
        h_buf[pl.ds(j, q, stride=nph), :] = h
    g = gate_ref[:, cs].astype(F32)
    o_ref[:, cs] = (h_buf[...] * (g * jax.nn.sigmoid(g))).astype(BF16)


_SCALE = HEAD_DIM ** -0.5
_ATTN_BATCH = 8


def _attend(q, k, v, has_prev):
    n, nk, _ = k.shape
    s = jnp.einsum("nqd,nkd->nqk", q, k, preferred_element_type=F32)
    qi = lax.broadcasted_iota(jnp.int32, (1, ATTN_BLOCK, nk), 1)
    kj = lax.broadcasted_iota(jnp.int32, (1, ATTN_BLOCK, nk), 2)
    if has_prev:
        d = kj - qi
        valid = (d >= 0) & (d <= ATTN_BLOCK)
    else:
        valid = kj <= qi
    s = jnp.where(valid, s, MASK_VALUE)
    m = jnp.max(s, axis=-1, keepdims=True)
    p = jnp.exp2((s - m) * (_SCALE * math.log2(math.e)))
    v_ones = jnp.concatenate([v, jnp.ones_like(v)], axis=-1)
    oe = jnp.einsum("nqk,nkd->nqd", p.astype(BF16), v_ones, preferred_element_type=F32)
    l = oe[:, :, HEAD_DIM:]
    o = oe[:, :, :HEAD_DIM] * (1.0 / l)
    return o, m * _SCALE + jnp.log(l)


def _attn_kernel(g0_ref, g1_ref, g2_ref, gate_ref, out_ref, o_scr, ld_scr):
    S = o_scr.shape[1]
    blk = ATTN_BLOCK

    def run_blocks(ref, bases, has_prev):
        res = []
        for i in range(0, len(bases), _ATTN_BATCH):
            chunk = bases[i:i + _ATTN_BATCH]
            lo = blk if has_prev else 0
            q = jnp.stack([ref[0, 0, 0, b:b + blk, :] for b in chunk])
            k = jnp.stack([ref[1, 0, 0, b - lo:b + blk, :] for b in chunk])
            v = jnp.stack([ref[2, 0, 0, b - lo:b + blk, :] for b in chunk])
            o, ld = _attend(q, k, v, has_prev)
            res += [(o[j], ld[j]) for j in range(len(chunk))]
        return res

    for gi, (ref, dil) in enumerate(zip((g0_ref, g1_ref, g2_ref), DILATIONS)):
        L = S // dil
        for has_prev in (False, True):
            blocks = [(r, n) for r in range(dil) for n in range(L // blk) if (n > 0) == has_prev]
            if not blocks:
                continue
            outs = run_blocks(ref, [r * L + n * blk for r, n in blocks], has_prev)
            for (r, n), (o, ld) in zip(blocks, outs):
                if dil == 1:
                    rows = pl.ds(n * blk, blk)
                else:
                    rows = pl.ds(n * blk * dil + r, blk, stride=dil)
                o_scr[gi, rows, :] = o
                ld_scr[gi, rows, :] = ld

    ch = 256

    def merge(c, _):
        rows = pl.ds(pl.multiple_of(c * ch, ch), ch)
        l0, l1, l2 = ld_scr[0, rows, :], ld_scr[1, rows, :], ld_scr[2, rows, :]
        m = jnp.maximum(jnp.maximum(l0, l1), l2)
        e0, e1, e2 = jnp.exp(l0 - m), jnp.exp(l1 - m), jnp.exp(l2 - m)
        num = e0 * o_scr[0, rows, :] + e1 * o_scr[1, rows, :] + e2 * o_scr[2, rows, :]
        o = num * (1.0 / (e0 + e1 + e2))
        g = gate_ref[0, rows, :].astype(F32)
        out_ref[0, rows, :] = (o * (g * jax.nn.sigmoid(g))).astype(BF16)
        return 0

    lax.fori_loop(0, S // ch, merge, 0)


def _attention(qkv, p_nat, B, S):
    H = HEADS_PER_GROUP
    qkv_spec = lambda: pl.BlockSpec((3, 1, 1, S, HEAD_DIM), lambda b, h: (0, b, h, 0, 0))
    return pl.pallas_call(
        _attn_kernel,
        grid=(B, H),
        in_specs=[qkv_spec(), qkv_spec(), qkv_spec(),
                  pl.BlockSpec((1, S, HEAD_DIM),
                               lambda b, h: (b, 0, _P_GATE_BLK * _COL_BLK // HEAD_DIM + h))],
        out_specs=pl.BlockSpec((1, S, HEAD_DIM), lambda b, h: (b, 0, h)),
        out_shape=jax.ShapeDtypeStruct((B, S, H * HEAD_DIM), BF16),
        scratch_shapes=[pltpu.VMEM((3, S, HEAD_DIM), F32), pltpu.VMEM((3, S, HEAD_DIM), F32)],
        compiler_params=_cparams(2),
        name="dilated_attn",
    )(*qkv, p_nat)


_DOT_LANES = 256


def _tail_kernel(xr_ref, gr_ref, cw_ref, cb_ref, wrg_ref, brg_ref, wig_ref, big_ref, lam_ref,
                 u_attn_ref, g_rnn_ref, g_attn_ref, x_ref, w_rnn_ref, w_attn_ref, w_o_ref,
                 g_post_ref, o_ref, xpad, stage, h_scr, carry, u_scr, merged_scr, y_scr, *,
                 tiles_per_seq):
    s = pl.program_id(0)

    @pl.when(s == 0)
    def _():
        u_scr[...] = jnp.zeros(u_scr.shape, BF16)

    @pl.when(s % tiles_per_seq == 0)
    def _():
        xpad[:, 0:_PAD, :] = jnp.zeros((RNN_BLOCKS, _PAD, RNN_BLOCK_W), F32)
        carry[...] = jnp.zeros(carry.shape, F32)

    def body(u_next, u_cur):
        n_chunks = D_MODEL // _DOT_LANES
        gates = lambda c: _rglru_gates(c, xr_ref, cw_ref, cb_ref, wrg_ref, wig_ref, xpad,
                                       stage.at[c % 2])
        gates(0)
        for c in range(RNN_BLOCKS):
            if c + 1 < RNN_BLOCKS:
                gates(c + 1)
            _rglru_scan(c, stage.at[c % 2], gr_ref, brg_ref, big_ref, lam_ref, u_next,
                        h_scr.at[c % 2], carry)
            lanes = slice((c % n_chunks) * _DOT_LANES, (c % n_chunks + 1) * _DOT_LANES)
            if c < n_chunks:
                y_rnn = jnp.dot(u_cur[...], w_rnn_ref[:, lanes], preferred_element_type=F32)
                y_attn = jnp.dot(u_attn_ref[...], w_attn_ref[:, lanes],
                                 preferred_element_type=F32)
                merged_scr[:, lanes] = (
                    jax.nn.sigmoid(g_rnn_ref[:, lanes].astype(F32)) * y_rnn
                    + jax.nn.sigmoid(g_attn_ref[:, lanes].astype(F32)) * y_attn).astype(BF16)
            else:
                y_scr[:, lanes] = jnp.dot(merged_scr[...], w_o_ref[:, lanes],
                                          preferred_element_type=F32)
        y = y_scr[...]
        ms = jnp.mean(y * y, axis=-1, keepdims=True)
        o_ref[...] = x_ref[...] + y * lax.rsqrt(ms + NORM_EPS) * g_post_ref[...]

    for parity in range(2):
        pl.when(s % 2 == parity)(
            functools.partial(body, u_scr.at[parity], u_scr.at[1 - parity]))


def _tail(p_nat, u_attn, x2d, conv_w, conv_b, w_rg, b_rg, w_ig, b_ig, lam,
          w_rnn_b, w_attn_b, w_o_b, g_post, S):
    M, D = x2d.shape
    tm, C, W = _RNN_TS, D_RNN, RNN_BLOCK_W
    n_tiles = M // tm
    assert RNN_BLOCKS == 2 * (D // _DOT_LANES)
    nxt = lambda s: jnp.minimum(s, n_tiles - 1)
    cur = lambda s: jnp.maximum(s - 1, 0)
    const = lambda shape: pl.BlockSpec(shape, lambda s: (0,) * len(shape))
    resident = lambda shape: pl.BlockSpec(shape, lambda s: (0, 0), pipeline_mode=pl.Buffered(1))
    return pl.pallas_call(
        functools.partial(_tail_kernel, tiles_per_seq=S // tm),
        grid=(n_tiles + 1,),
        in_specs=[pl.BlockSpec((tm, C), lambda s: (nxt(s), 0)),
                  pl.BlockSpec((tm, C), lambda s: (nxt(s), 1)),
                  const((CONV_W, C)), const((1, C)), const((RNN_BLOCKS, W, W)), const((1, C)),
                  const((RNN_BLOCKS, W, W)), const((1, C)), const((1, C)),
                  pl.BlockSpec((tm, u_attn.shape[1]), lambda s: (cur(s), 0)),
                  pl.BlockSpec((tm, D), lambda s: (cur(s), _P_GRNN_BLK * _COL_BLK // D)),
                  pl.BlockSpec((tm, D), lambda s: (cur(s), _P_GATTN_BLK * _COL_BLK // D)),
                  pl.BlockSpec((tm, D), lambda s: (cur(s), 0)),
                  resident(w_rnn_b.shape), resident(w_attn_b.shape), resident(w_o_b.shape),
                  const((1, D))],
        out_specs=pl.BlockSpec((tm, D), lambda s: (cur(s), 0)),
        out_shape=jax.ShapeDtypeStruct((M, D), F32),
        scratch_shapes=[pltpu.VMEM((RNN_BLOCKS, tm + _PAD, W), F32),
                        pltpu.VMEM((2, 3, tm, W), F32),
                        pltpu.VMEM((2, tm, W), F32),
                        pltpu.VMEM((RNN_BLOCKS, 1, W), F32),
                        pltpu.VMEM((2, tm, C), BF16),
                        pltpu.VMEM((tm, D), BF16),
                        pltpu.VMEM((tm, D), F32)],
        compiler_params=_cparams(1),
        name="rglru_output",
    )(p_nat, p_nat, conv_w, conv_b.reshape(1, C), w_rg, b_rg.reshape(1, C), w_ig,
      b_ig.reshape(1, C), lam.reshape(1, C), u_attn, p_nat, p_nat, x2d,
      w_rnn_b, w_attn_b, w_o_b, g_post.reshape(1, D))


def _layer(x, ln_pre_g, w_in, conv_w, conv_b, w_rg, b_rg, w_ig, b_ig, lru_lambda,
           w_rnn_out, w_attn_out, w_o, ln_post_g):
    B, S, D = x.shape
    M = B * S
    x2d = x.reshape(M, D)
    h = _prenorm(x2d, ln_pre_g)
    p_nat, w_rnn_b, w_attn_b, w_o_b = _proj_nat(h, w_in, (w_rnn_out, w_attn_out, w_o))
    qkv = [_proj_qkv(h, w_in, _rope_tables(S, dil), g, B, S) for g, dil in enumerate(DILATIONS)]
    u_attn = _attention(qkv, p_nat.reshape(B, S, p_nat.shape[1]), B, S)
    out = _tail(p_nat, u_attn.reshape(M, -1), x2d, conv_w, conv_b, w_rg, b_rg, w_ig, b_ig,
                lru_lambda, w_rnn_b, w_attn_b, w_o_b, ln_post_g, S)
    return out.reshape(B, S, D)


def kernel(x, ln_pre_g, w_in, conv_w, conv_b, w_rg, b_rg, w_ig, b_ig, lru_lambda,
           w_rnn_out, w_attn_out, w_o, ln_post_g):
    for layer in range(ln_pre_g.shape[0]):
        x = _layer(x, ln_pre_g[layer], w_in[layer], conv_w[layer], conv_b[layer], w_rg[layer],
                   b_rg[layer], w_ig[layer], b_ig[layer], lru_lambda[layer], w_rnn_out[layer],
                   w_attn_out[layer], w_o[layer], ln_post_g[layer])
    return x
```

```python
import functools
import math

import jax
import jax.numpy as jnp
import numpy as np
from jax import lax
from jax.experimental import pallas as pl
from jax.experimental.pallas import tpu as pltpu

D_MODEL = 2048
D_RNN = 2048
RNN_BLOCKS = 16
RNN_BLOCK_W = 128
CONV_W = 4
LRU_C = 8.0
HEAD_DIM = 128
HEADS_PER_GROUP = 8
DILATIONS = (1, 4, 16)
ATTN_BLOCK = 128
ROPE_THETA = 500000.0
ROT_DIM = 32
NORM_EPS = 1e-6
MASK_VALUE = -1e30

BF16 = jnp.bfloat16
F32 = jnp.float32

_COL_BLK = 1024
_Q_BLK0 = 4
_NAT_BLOCKS = 9
_P_GRNN_BLK = 4
_P_GATTN_BLK = 6
_P_GATE_BLK = 8


def _nat_w_block(j):
    return jnp.where(j < 4, j, jnp.where(j < 8, j + 10, 13))

_HEADS_PER_DOT = 2

_VMEM_LIMIT = 56 * 1024 * 1024


def _cparams(n_axes):
    return pltpu.CompilerParams(
        dimension_semantics=("arbitrary",) * n_axes, vmem_limit_bytes=_VMEM_LIMIT)


_NORM_T = 512


def _prenorm_kernel(x_ref, g_ref, h_ref):
    x = x_ref[...]
    ms = jnp.mean(x * x, axis=-1, keepdims=True)
    h_ref[...] = (x * lax.rsqrt(ms + NORM_EPS) * g_ref[...]).astype(BF16)


def _prenorm(x2d, g):
    M, D = x2d.shape
    T = _NORM_T
    return pl.pallas_call(
        _prenorm_kernel,
        grid=(M // T,),
        in_specs=[pl.BlockSpec((T, D), lambda i: (i, 0)),
                  pl.BlockSpec((1, D), lambda i: (0, 0))],
        out_specs=pl.BlockSpec((T, D), lambda i: (i, 0)),
        out_shape=jax.ShapeDtypeStruct((M, D), BF16),
        compiler_params=_cparams(1),
        name="prenorm",
    )(x2d, g.reshape(1, D))


def _cast_weight_once(w_ref, wb_ref):
    @pl.when(pl.program_id(1) == 0)
    def _():
        wb_ref[...] = w_ref[...].astype(BF16)


_CAST_ROWS = 256


def _proj_nat_kernel(lhs_ref, w_ref, *rest, cast_steps):
    n = len(cast_steps)
    wf_refs, o_ref, wo_refs, wb_ref = rest[:n], rest[n], rest[n + 1:2 * n + 1], rest[2 * n + 1]
    _cast_weight_once(w_ref, wb_ref)
    step = pl.program_id(0) * pl.num_programs(1) + pl.program_id(1)
    for wf_ref, wo_ref, steps in zip(wf_refs, wo_refs, cast_steps):
        @pl.when(step < steps)
        def _(wf_ref=wf_ref, wo_ref=wo_ref):
            wo_ref[...] = wf_ref[...].astype(BF16)
    o_ref[...] = jnp.dot(lhs_ref[...], wb_ref[...], preferred_element_type=F32).astype(BF16)


def _proj_nat(h, w_in, out_weights):
    M, K = h.shape
    tm, tn = 1024, _COL_BLK
    row_tiles = M // tm
    cast_steps = tuple(w.shape[0] // _CAST_ROWS for w in out_weights)
    assert max(cast_steps) <= _NAT_BLOCKS * row_tiles

    def chunk_spec(w, steps):
        return pl.BlockSpec((_CAST_ROWS, w.shape[1]),
                            lambda j, i: (jnp.minimum(j * row_tiles + i, steps - 1), 0))

    return pl.pallas_call(
        functools.partial(_proj_nat_kernel, cast_steps=cast_steps),
        grid=(_NAT_BLOCKS, row_tiles),
        in_specs=[pl.BlockSpec((tm, K), lambda j, i: (i, 0)),
                  pl.BlockSpec((K, tn), lambda j, i: (0, _nat_w_block(j)))]
        + [chunk_spec(w, n) for w, n in zip(out_weights, cast_steps)],
        out_specs=[pl.BlockSpec((tm, tn), lambda j, i: (i, j))]
        + [chunk_spec(w, n) for w, n in zip(out_weights, cast_steps)],
        out_shape=[jax.ShapeDtypeStruct((M, _NAT_BLOCKS * tn), BF16)]
        + [jax.ShapeDtypeStruct(w.shape, BF16) for w in out_weights],
        scratch_shapes=[pltpu.VMEM((K, tn), BF16)],
        compiler_params=_cparams(2),
        name="proj_nat",
    )(h, w_in, *out_weights)


_MAX_ROW_STRIDE = 4


def _proj_qkv_kernel(lhs_ref, w_ref, cos_ref, sa_ref, sb_ref, o_ref, wb_ref, acc_scr, *rest,
                     row_tiles, n_tiles, dil):
    t = pl.program_id(0)
    tm = acc_scr.shape[2]
    rows = tm // dil

    def class_rows(acc_prev, c):
        if dil == 1:
            return {0: acc_prev[c]}
        if dil <= _MAX_ROW_STRIDE:
            return {r: acc_prev[c, pl.ds(r, rows, stride=dil), :] for r in range(dil)}
        (tmp_scr,) = rest
        s0 = _MAX_ROW_STRIDE
        s1 = dil // s0
        assert s1 <= _MAX_ROW_STRIDE
        for r0 in range(s0):
            tmp_scr[c, r0] = acc_prev[c, pl.ds(r0, tm // s0, stride=s0), :]
        return {s0 * r1 + r0: tmp_scr[c, r0, pl.ds(r1, rows, stride=s1), :]
                for r0 in range(s0) for r1 in range(s1)}

    @pl.when(t == 0)
    def _():
        acc_scr[...] = jnp.zeros(acc_scr.shape, F32)

    @pl.when((t % row_tiles == 0) & (t < n_tiles))
    def _():
        wb_ref[...] = w_ref[...].astype(BF16)

    def body(acc_cur, acc_prev):
        is_qk = (t - 1) // row_tiles < 2
        cos = jnp.where(is_qk, cos_ref[...], 1.0)
        sa = jnp.where(is_qk, sa_ref[...], 0.0)
        sb = jnp.where(is_qk, sb_ref[...], 0.0)
        for c0 in range(0, HEADS_PER_GROUP, _HEADS_PER_DOT):
            lanes = slice(c0 * HEAD_DIM, (c0 + _HEADS_PER_DOT) * HEAD_DIM)
            acc = jnp.dot(lhs_ref[...], wb_ref[:, lanes], preferred_element_type=F32)
            for c in range(c0, c0 + _HEADS_PER_DOT):
                acc_cur[c] = acc[:, (c - c0) * HEAD_DIM:(c - c0 + 1) * HEAD_DIM]
            for c in range(c0, c0 + _HEADS_PER_DOT):
                for r, v in class_rows(acc_prev, c).items():
                    rot = (v * cos[r] + pltpu.roll(v, HEAD_DIM - ROT_DIM // 2, 1) * sa[r]
                           + pltpu.roll(v, ROT_DIM // 2, 1) * sb[r])
                    o_ref[0, 0, c, r] = rot.astype(BF16)

    for parity in range(2):
        pl.when(t % 2 == parity)(
            functools.partial(body, acc_scr.at[parity], acc_scr.at[1 - parity]))


def _proj_qkv(h, w_in, tables, group, B, S):
    M, K = h.shape
    dil = DILATIONS[group]
    L = S // dil
    tm, tn = 1024, _COL_BLK
    spt = S // tm
    row_tiles = M // tm
    n_tiles = 3 * row_tiles
    mm = lambda t: jnp.minimum(t, n_tiles - 1)
    ep = lambda t: jnp.maximum(t - 1, 0)
    tbl_spec = pl.BlockSpec((dil, tm // dil, HEAD_DIM), lambda t: (0, ep(t) % row_tiles % spt, 0))
    out = pl.pallas_call(
        functools.partial(_proj_qkv_kernel, row_tiles=row_tiles, n_tiles=n_tiles, dil=dil),
        grid=(n_tiles + 1,),
        in_specs=[pl.BlockSpec((tm, K), lambda t: (mm(t) % row_tiles, 0)),
                  pl.BlockSpec((K, tn),
                               lambda t: (0, _Q_BLK0 + group + 3 * (mm(t) // row_tiles))),
                  tbl_spec, tbl_spec, tbl_spec],
        out_specs=pl.BlockSpec(
            (1, 1, HEADS_PER_GROUP, dil, tm // dil, HEAD_DIM),
            lambda t: (ep(t) // row_tiles, ep(t) % row_tiles // spt, 0, 0,
                       ep(t) % row_tiles % spt, 0)),
        out_shape=jax.ShapeDtypeStruct((3, B, HEADS_PER_GROUP, dil, L, HEAD_DIM), BF16),
        scratch_shapes=[pltpu.VMEM((K, tn), BF16),
                        pltpu.VMEM((2, HEADS_PER_GROUP, tm, HEAD_DIM), F32)]
        + ([pltpu.VMEM((HEADS_PER_GROUP, _MAX_ROW_STRIDE, tm // _MAX_ROW_STRIDE, HEAD_DIM), F32)]
           if dil > _MAX_ROW_STRIDE else []),
        compiler_params=_cparams(1),
        name=f"proj_qkv_g{group}",
    )(h, w_in, *[tbl.reshape(dil, L, HEAD_DIM) for tbl in tables])
    return out.reshape(3, B, HEADS_PER_GROUP, S, HEAD_DIM)


def _rope_tables(S, dilation):
    half = ROT_DIM // 2
    inv = np.exp(-math.log(ROPE_THETA) * np.arange(half, dtype=np.float32)
                 * np.float32(2.0 / ROT_DIM)).astype(np.float32)
    L = S // dilation
    p = np.arange(S)
    pos = (p % L) * dilation + p // L
    ang = pos.astype(np.float32)[:, None] * inv[None, :]
    cos, sin = np.cos(ang).astype(np.float32), np.sin(ang).astype(np.float32)
    cos_t = np.ones((S, HEAD_DIM), np.float32)
    sin_a = np.zeros((S, HEAD_DIM), np.float32)
    sin_b = np.zeros((S, HEAD_DIM), np.float32)
    cos_t[:, :half] = cos
    cos_t[:, half:ROT_DIM] = cos
    sin_a[:, :half] = -sin
    sin_b[:, half:ROT_DIM] = sin
    return jnp.asarray(cos_t), jnp.asarray(sin_a), jnp.asarray(sin_b)


_RNN_TS = 256
_PAD = 8


def _softplus(z):
    return jnp.maximum(z, 0.0) + jnp.log1p(jnp.exp(-jnp.abs(z)))


def _scan8(a, b, row):
    for k in (1, 2, 4):
        a_sh = jnp.where(row >= k, pltpu.roll(a, k, 0), 1.0)
        b_sh = jnp.where(row >= k, pltpu.roll(b, k, 0), 0.0)
        b = a * b_sh + b
        a = a * a_sh
    return a, b


_RNN_PHASES = 4


def _rglru_gates(c, x_ref, cw_ref, cb_ref, wrg_ref, wig_ref, xpad, stage):
    ts, nph = _RNN_TS, _RNN_PHASES
    q = ts // nph
    W = RNN_BLOCK_W
    cs = slice(c * W, (c + 1) * W)
    xpad[c, _PAD:_PAD + ts, :] = x_ref[:, cs].astype(F32)
    xs = {off: xpad[c, pl.ds(_PAD + off, q, stride=nph), :]
          for off in range(-(CONV_W - 1), nph)}
    w = [cw_ref[k:k + 1, cs] for k in range(CONV_W)]
    parts = []
    for j in range(nph):
        acc = cb_ref[:, cs] + w[0] * xs[j - (CONV_W - 1)]
        for k in range(1, CONV_W):
            acc = acc + w[k] * xs[j - (CONV_W - 1) + k]
        parts.append(acc)
    xc = jnp.concatenate(parts, axis=0)
    xb = xc.astype(BF16)
    xpad[c, 0:_PAD, :] = xpad[c, ts:ts + _PAD, :]
    stage[0] = xc
    stage[1] = jnp.dot(xb, wrg_ref[c].astype(BF16), preferred_element_type=F32)
    stage[2] = jnp.dot(xb, wig_ref[c].astype(BF16), preferred_element_type=F32)


def _rglru_scan(c, stage, gate_ref, brg_ref, big_ref, lam_ref, o_ref, h_buf, carry):
    ts, nph = _RNN_TS, _RNN_PHASES
    q = ts // nph
    W = RNN_BLOCK_W
    row = lax.broadcasted_iota(jnp.int32, (8, W), 0)
    cs = slice(c * W, (c + 1) * W)
    xc = stage[0]
    r = jax.nn.sigmoid(stage[1] + brg_ref[:, cs])
    ig = jax.nn.sigmoid(stage[2] + big_ref[:, cs])
    log_a = (-LRU_C * _softplus(-lam_ref[:, cs])) * r
    a = jnp.exp(log_a)
    om = 1.0 - a * a
    stage[1] = a
    stage[2] = jnp.where(om > 0.0, om * lax.rsqrt(om), 0.0) * (ig * xc)
    part = lambda k, j: stage[k, j * q:(j + 1) * q, :]

    tot_a, tot_b = part(1, 0), part(2, 0)
    for j in range(1, nph):
        tot_b = part(1, j) * tot_b + part(2, j)
        tot_a = part(1, j) * tot_a
    h_in = carry[c]
    seg_in = []
    for m in range(q // 8):
        pa, pb = _scan8(tot_a[m * 8:(m + 1) * 8], tot_b[m * 8:(m + 1) * 8], row)
        after = pa * h_in + pb
        seg_in.append(jnp.where(row >= 1, pltpu.roll(after, 1, 0), h_in))
        h_in = after[7:8, :]
    carry[c] = h_in
    h = jnp.concatenate(seg_in, axis=0)
    for j in range(nph):
        h = part(1, j) * h + part(2, j)
        h_buf[pl.ds(j, q, stride=nph), :] = h
    g = gate_ref[:, cs].astype(F32)
    o_ref[:, cs] = (h_buf[...] * (g * jax.nn.sigmoid(g))).astype(BF16)


_SCALE = HEAD_DIM ** -0.5
_ATTN_BATCH = 16


def _attend(q, k, v, has_prev):
    n, nk, _ = k.shape
    s = jnp.einsum("nqd,nkd->nqk", q, k, preferred_element_type=F32)
    qi = lax.broadcasted_iota(jnp.int32, (1, ATTN_BLOCK, nk), 1)
    kj = lax.broadcasted_iota(jnp.int32, (1, ATTN_BLOCK, nk), 2)
    if has_prev:
        d = kj - qi
        valid = (d >= 0) & (d <= ATTN_BLOCK)
    else:
        valid = kj <= qi
    s = jnp.where(valid, s, MASK_VALUE)
    m = jnp.max(s, axis=-1, keepdims=True)
    p = jnp.exp2((s - m) * (_SCALE * math.log2(math.e)))
    v_ones = jnp.concatenate([v, jnp.ones_like(v)], axis=-1)
    oe = jnp.einsum("nqk,nkd->nqd", p.astype(BF16), v_ones, preferred_element_type=F32)
    l = oe[:, :, HEAD_DIM:]
    o = oe[:, :, :HEAD_DIM] * (1.0 / l)
    return o, m * _SCALE + jnp.log(l)


def _attn_kernel(g0_ref, g1_ref, g2_ref, gate_ref, out_ref, o_scr, ld_scr):
    S = o_scr.shape[1]
    blk = ATTN_BLOCK

    def run_blocks(ref, bases, has_prev):
        res = []
        for i in range(0, len(bases), _ATTN_BATCH):
            chunk = bases[i:i + _ATTN_BATCH]
            lo = blk if has_prev else 0
            q = jnp.stack([ref[0, 0, 0, b:b + blk, :] for b in chunk])
            k = jnp.stack([ref[1, 0, 0, b - lo:b + blk, :] for b in chunk])
            v = jnp.stack([ref[2, 0, 0, b - lo:b + blk, :] for b in chunk])
            o, ld = _attend(q, k, v, has_prev)
            res += [(o[j], ld[j]) for j in range(len(chunk))]
        return res

    for gi, (ref, dil) in enumerate(zip((g0_ref, g1_ref, g2_ref), DILATIONS)):
        L = S // dil
        for has_prev in (False, True):
            blocks = [(r, n) for r in range(dil) for n in range(L // blk) if (n > 0) == has_prev]
            if not blocks:
                continue
            outs = run_blocks(ref, [r * L + n * blk for r, n in blocks], has_prev)
            for (r, n), (o, ld) in zip(blocks, outs):
                if dil == 1:
                    rows = pl.ds(n * blk, blk)
                else:
                    rows = pl.ds(n * blk * dil + r, blk, stride=dil)
                o_scr[gi, rows, :] = o
                ld_scr[gi, rows, :] = ld

    ch = 256

    def merge(c, _):
        rows = pl.ds(pl.multiple_of(c * ch, ch), ch)
        l0, l1, l2 = ld_scr[0, rows, :], ld_scr[1, rows, :], ld_scr[2, rows, :]
        m = jnp.maximum(jnp.maximum(l0, l1), l2)
        e0, e1, e2 = jnp.exp(l0 - m), jnp.exp(l1 - m), jnp.exp(l2 - m)
        num = e0 * o_scr[0, rows, :] + e1 * o_scr[1, rows, :] + e2 * o_scr[2, rows, :]
        o = num * (1.0 / (e0 + e1 + e2))
        g = gate_ref[0, rows, :].astype(F32)
        out_ref[0, rows, :] = (o * (g * jax.nn.sigmoid(g))).astype(BF16)
        return 0

    lax.fori_loop(0, S // ch, merge, 0)


def _attention(qkv, p_nat, B, S):
    H = HEADS_PER_GROUP
    qkv_spec = lambda: pl.BlockSpec((3, 1, 1, S, HEAD_DIM), lambda b, h: (0, b, h, 0, 0))
    return pl.pallas_call(
        _attn_kernel,
        grid=(B, H),
        in_specs=[qkv_spec(), qkv_spec(), qkv_spec(),
                  pl.BlockSpec((1, S, HEAD_DIM),
                               lambda b, h: (b, 0, _P_GATE_BLK * _COL_BLK // HEAD_DIM + h))],
        out_specs=pl.BlockSpec((1, S, HEAD_DIM), lambda b, h: (b, 0, h)),
        out_shape=jax.ShapeDtypeStruct((B, S, H * HEAD_DIM), BF16),
        scratch_shapes=[pltpu.VMEM((3, S, HEAD_DIM), F32), pltpu.VMEM((3, S, HEAD_DIM), F32)],
        compiler_params=_cparams(2),
        name="dilated_attn",
    )(*qkv, p_nat)


_DOT_LANES = 256


def _tail_kernel(xr_ref, gr_ref, cw_ref, cb_ref, wrg_ref, brg_ref, wig_ref, big_ref, lam_ref,
                 u_attn_ref, g_rnn_ref, g_attn_ref, x_ref, w_rnn_ref, w_attn_ref, w_o_ref,
                 g_post_ref, o_ref, xpad, stage, h_scr, carry, u_scr, merged_scr, y_scr, *,
                 tiles_per_seq):
    s = pl.program_id(0)

    @pl.when(s == 0)
    def _():
        u_scr[...] = jnp.zeros(u_scr.shape, BF16)

    @pl.when(s % tiles_per_seq == 0)
    def _():
        xpad[:, 0:_PAD, :] = jnp.zeros((RNN_BLOCKS, _PAD, RNN_BLOCK_W), F32)
        carry[...] = jnp.zeros(carry.shape, F32)

    def body(u_next, u_cur):
        n_chunks = D_MODEL // _DOT_LANES
        gates = lambda c: _rglru_gates(c, xr_ref, cw_ref, cb_ref, wrg_ref, wig_ref, xpad,
                                       stage.at[c % 2])
        gates(0)
        for c in range(RNN_BLOCKS):
            if c + 1 < RNN_BLOCKS:
                gates(c + 1)
            _rglru_scan(c, stage.at[c % 2], gr_ref, brg_ref, big_ref, lam_ref, u_next,
                        h_scr.at[c % 2], carry)
            lanes = slice((c % n_chunks) * _DOT_LANES, (c % n_chunks + 1) * _DOT_LANES)
            if c < n_chunks:
                y_rnn = jnp.dot(u_cur[...], w_rnn_ref[:, lanes], preferred_element_type=F32)
                y_attn = jnp.dot(u_attn_ref[...], w_attn_ref[:, lanes],
                                 preferred_element_type=F32)
                merged_scr[:, lanes] = (
                    jax.nn.sigmoid(g_rnn_ref[:, lanes].astype(F32)) * y_rnn
                    + jax.nn.sigmoid(g_attn_ref[:, lanes].astype(F32)) * y_attn).astype(BF16)
            else:
                y_scr[:, lanes] = jnp.dot(merged_scr[...], w_o_ref[:, lanes],
                                          preferred_element_type=F32)
        y = y_scr[...]
        ms = jnp.mean(y * y, axis=-1, keepdims=True)
        o_ref[...] = x_ref[...] + y * lax.rsqrt(ms + NORM_EPS) * g_post_ref[...]

    for parity in range(2):
        pl.when(s % 2 == parity)(
            functools.partial(body, u_scr.at[parity], u_scr.at[1 - parity]))


def _tail(p_nat, u_attn, x2d, conv_w, conv_b, w_rg, b_rg, w_ig, b_ig, lam,
          w_rnn_b, w_attn_b, w_o_b, g_post, S):
    M, D = x2d.shape
    tm, C, W = _RNN_TS, D_RNN, RNN_BLOCK_W
    n_tiles = M // tm
    assert RNN_BLOCKS == 2 * (D // _DOT_LANES)
    nxt = lambda s: jnp.minimum(s, n_tiles - 1)
    cur = lambda s: jnp.maximum(s - 1, 0)
    const = lambda shape: pl.BlockSpec(shape, lambda s: (0,) * len(shape))
    resident = lambda shape: pl.BlockSpec(shape, lambda s: (0, 0), pipeline_mode=pl.Buffered(1))
    return pl.pallas_call(
        functools.partial(_tail_kernel, tiles_per_seq=S // tm),
        grid=(n_tiles + 1,),
        in_specs=[pl.BlockSpec((tm, C), lambda s: (nxt(s), 0)),
                  pl.BlockSpec((tm, C), lambda s: (nxt(s), 1)),
                  const((CONV_W, C)), const((1, C)), const((RNN_BLOCKS, W, W)), const((1, C)),
                  const((RNN_BLOCKS, W, W)), const((1, C)), const((1, C)),
                  pl.BlockSpec((tm, u_attn.shape[1]), lambda s: (cur(s), 0)),
                  pl.BlockSpec((tm, D), lambda s: (cur(s), _P_GRNN_BLK * _COL_BLK // D)),
                  pl.BlockSpec((tm, D), lambda s: (cur(s), _P_GATTN_BLK * _COL_BLK // D)),
                  pl.BlockSpec((tm, D), lambda s: (cur(s), 0)),
                  resident(w_rnn_b.shape), resident(w_attn_b.shape), resident(w_o_b.shape),
                  const((1, D))],
        out_specs=pl.BlockSpec((tm, D), lambda s: (cur(s), 0)),
        out_shape=jax.ShapeDtypeStruct((M, D), F32),
        scratch_shapes=[pltpu.VMEM((RNN_BLOCKS, tm + _PAD, W), F32),
                        pltpu.VMEM((2, 3, tm, W), F32),
                        pltpu.VMEM((2, tm, W), F32),
                        pltpu.VMEM((RNN_BLOCKS, 1, W), F32),
                        pltpu.VMEM((2, tm, C), BF16),
                        pltpu.VMEM((tm, D), BF16),
                        pltpu.VMEM((tm, D), F32)],
        compiler_params=_cparams(1),
        name="rglru_output",
    )(p_nat, p_nat, conv_w, conv_b.reshape(1, C), w_rg, b_rg.reshape(1, C), w_ig,
      b_ig.reshape(1, C), lam.reshape(1, C), u_attn, p_nat, p_nat, x2d,
      w_rnn_b, w_attn_b, w_o_b, g_post.reshape(1, D))


def _layer(x, ln_pre_g, w_in, conv_w, conv_b, w_rg, b_rg, w_ig, b_ig, lru_lambda,
           w_rnn_out, w_attn_out, w_o, ln_post_g):
    B, S, D = x.shape
    M = B * S
    x2d = x.reshape(M, D)
    h = _prenorm(x2d, ln_pre_g)
    p_nat, w_rnn_b, w_attn_b, w_o_b = _proj_nat(h, w_in, (w_rnn_out, w_attn_out, w_o))
    qkv = [_proj_qkv(h, w_in, _rope_tables(S, dil), g, B, S) for g, dil in enumerate(DILATIONS)]
    u_attn = _attention(qkv, p_nat.reshape(B, S, p_nat.shape[1]), B, S)
    out = _tail(p_nat, u_attn.reshape(M, -1), x2d, conv_w, conv_b, w_rg, b_rg, w_ig, b_ig,
                lru_lambda, w_rnn_b, w_attn_b, w_o_b, ln_post_g, S)
    return out.reshape(B, S, D)


def kernel(x, ln_pre_g, w_in, conv_w, conv_b, w_rg, b_rg, w_ig, b_ig, lru_lambda,
           w_rnn_out, w_attn_out, w_o, ln_post_g):
    for layer in range(ln_pre_g.shape[0]):
        x = _layer(x, ln_pre_g[layer], w_in[layer], conv_w[layer], conv_b[layer], w_rg[layer],
                   b_rg[layer], w_ig[layer], b_ig[layer], lru_lambda[layer], w_rnn_out[layer],
                   w_attn_out[layer], w_o[layer], ln_post_g[layer])
    return x
```

```python
import functools
import math

import jax
import jax.numpy as jnp
import numpy as np
from jax import lax
from jax.experimental import pallas as pl
from jax.experimental.pallas import tpu as pltpu

D_MODEL = 2048
D_RNN = 2048
RNN_BLOCKS = 16
RNN_BLOCK_W = 128
CONV_W = 4
LRU_C = 8.0
HEAD_DIM = 128
HEADS_PER_GROUP = 8
DILATIONS = (1, 4, 16)
ATTN_BLOCK = 128
ROPE_THETA = 500000.0
ROT_DIM = 32
NORM_EPS = 1e-6
MASK_VALUE = -1e30

BF16 = jnp.bfloat16
F32 = jnp.float32

_COL_BLK = 1024
_Q_BLK0 = 4
_NAT_BLOCKS = 7
_P_RGATE_BLK = 0
_P_GRNN_BLK = 2
_P_GATTN_BLK = 4
_P_GATE_BLK = 6


def _nat_w_block(j):
    return jnp.where(j < 2, j + 2, jnp.where(j < 6, j + 12, 13))

_HEADS_PER_DOT = 2

_VMEM_LIMIT = 56 * 1024 * 1024


def _cparams(n_axes):
    return pltpu.CompilerParams(
        dimension_semantics=("arbitrary",) * n_axes, vmem_limit_bytes=_VMEM_LIMIT)


_NORM_T = 512


def _prenorm_kernel(x_ref, g_ref, w_ref, h_ref, px_ref, wb_ref):
    @pl.when(pl.program_id(0) == 0)
    def _():
        wb_ref[...] = w_ref[...].astype(BF16)

    x = x_ref[...]
    ms = jnp.mean(x * x, axis=-1, keepdims=True)
    h = (x * lax.rsqrt(ms + NORM_EPS) * g_ref[...]).astype(BF16)
    h_ref[...] = h
    px_ref[...] = jnp.dot(h, wb_ref[...], preferred_element_type=F32).astype(BF16)


def _prenorm(x2d, g, w_in):
    M, D = x2d.shape
    T = _NORM_T
    return pl.pallas_call(
        _prenorm_kernel,
        grid=(M // T,),
        in_specs=[pl.BlockSpec((T, D), lambda i: (i, 0)),
                  pl.BlockSpec((1, D), lambda i: (0, 0)),
                  pl.BlockSpec((D, D_RNN), lambda i: (0, 0), pipeline_mode=pl.Buffered(1))],
        out_specs=[pl.BlockSpec((T, D), lambda i: (i, 0)),
                   pl.BlockSpec((T, D_RNN), lambda i: (i, 0))],
        out_shape=[jax.ShapeDtypeStruct((M, D), BF16), jax.ShapeDtypeStruct((M, D_RNN), BF16)],
        scratch_shapes=[pltpu.VMEM((D, D_RNN), BF16)],
        compiler_params=_cparams(1),
        name="prenorm_proj_x",
    )(x2d, g.reshape(1, D), w_in)


def _cast_weight_once(w_ref, wb_ref):
    @pl.when(pl.program_id(1) == 0)
    def _():
        wb_ref[...] = w_ref[...].astype(BF16)


_CAST_ROWS = 256


def _proj_nat_kernel(lhs_ref, w_ref, *rest, cast_steps):
    n = len(cast_steps)
    wf_refs, o_ref, wo_refs, wb_ref = rest[:n], rest[n], rest[n + 1:2 * n + 1], rest[2 * n + 1]
    _cast_weight_once(w_ref, wb_ref)
    step = pl.program_id(0) * pl.num_programs(1) + pl.program_id(1)
    for wf_ref, wo_ref, steps in zip(wf_refs, wo_refs, cast_steps):
        @pl.when(step < steps)
        def _(wf_ref=wf_ref, wo_ref=wo_ref):
            wo_ref[...] = wf_ref[...].astype(BF16)
    o_ref[...] = jnp.dot(lhs_ref[...], wb_ref[...], preferred_element_type=F32).astype(BF16)


def _proj_nat(h, w_in, out_weights):
    M, K = h.shape
    tm, tn = 1024, _COL_BLK
    row_tiles = M // tm
    cast_steps = tuple(w.shape[0] // _CAST_ROWS for w in out_weights)
    assert max(cast_steps) <= _NAT_BLOCKS * row_tiles

    def chunk_spec(w, steps):
        return pl.BlockSpec((_CAST_ROWS, w.shape[1]),
                            lambda j, i: (jnp.minimum(j * row_tiles + i, steps - 1), 0))

    return pl.pallas_call(
        functools.partial(_proj_nat_kernel, cast_steps=cast_steps),
        grid=(_NAT_BLOCKS, row_tiles),
        in_specs=[pl.BlockSpec((tm, K), lambda j, i: (i, 0)),
                  pl.BlockSpec((K, tn), lambda j, i: (0, _nat_w_block(j)))]
        + [chunk_spec(w, n) for w, n in zip(out_weights, cast_steps)],
        out_specs=[pl.BlockSpec((tm, tn), lambda j, i: (i, j))]
        + [chunk_spec(w, n) for w, n in zip(out_weights, cast_steps)],
        out_shape=[jax.ShapeDtypeStruct((M, _NAT_BLOCKS * tn), BF16)]
        + [jax.ShapeDtypeStruct(w.shape, BF16) for w in out_weights],
        scratch_shapes=[pltpu.VMEM((K, tn), BF16)],
        compiler_params=_cparams(2),
        name="proj_nat",
    )(h, w_in, *out_weights)


_MAX_ROW_STRIDE = 4


def _proj_qkv_kernel(lhs_ref, w_ref, cos_ref, sa_ref, sb_ref, o_ref, wb_ref, acc_scr, *rest,
                     row_tiles, n_tiles, dil):
    t = pl.program_id(0)
    tm = acc_scr.shape[2]
    rows = tm // dil

    def class_rows(acc_prev, c):
        if dil == 1:
            return {0: acc_prev[c]}
        if dil <= _MAX_ROW_STRIDE:
            return {r: acc_prev[c, pl.ds(r, rows, stride=dil), :] for r in range(dil)}
        (tmp_scr,) = rest
        s0 = _MAX_ROW_STRIDE
        s1 = dil // s0
        assert s1 <= _MAX_ROW_STRIDE
        for r0 in range(s0):
            tmp_scr[c, r0] = acc_prev[c, pl.ds(r0, tm // s0, stride=s0), :]
        return {s0 * r1 + r0: tmp_scr[c, r0, pl.ds(r1, rows, stride=s1), :]
                for r0 in range(s0) for r1 in range(s1)}

    @pl.when(t == 0)
    def _():
        acc_scr[...] = jnp.zeros(acc_scr.shape, F32)

    @pl.when((t % row_tiles == 0) & (t < n_tiles))
    def _():
        wb_ref[...] = w_ref[...].astype(BF16)

    def body(acc_cur, acc_prev):
        is_qk = (t - 1) // row_tiles < 2
        cos = jnp.where(is_qk, cos_ref[...], 1.0)
        sa = jnp.where(is_qk, sa_ref[...], 0.0)
        sb = jnp.where(is_qk, sb_ref[...], 0.0)
        for c0 in range(0, HEADS_PER_GROUP, _HEADS_PER_DOT):
            lanes = slice(c0 * HEAD_DIM, (c0 + _HEADS_PER_DOT) * HEAD_DIM)
            if acc_cur is not None:
                acc = jnp.dot(lhs_ref[...], wb_ref[:, lanes], preferred_element_type=F32)
                for c in range(c0, c0 + _HEADS_PER_DOT):
                    acc_cur[c] = acc[:, (c - c0) * HEAD_DIM:(c - c0 + 1) * HEAD_DIM]
            for c in range(c0, c0 + _HEADS_PER_DOT):
                for r, v in class_rows(acc_prev, c).items():
                    rot = (v * cos[r] + pltpu.roll(v, HEAD_DIM - ROT_DIM // 2, 1) * sa[r]
                           + pltpu.roll(v, ROT_DIM // 2, 1) * sb[r])
                    o_ref[0, 0, c, r] = rot.astype(BF16)

    for parity in range(2):
        pl.when((t % 2 == parity) & (t < n_tiles))(
            functools.partial(body, acc_scr.at[parity], acc_scr.at[1 - parity]))
    pl.when(t == n_tiles)(functools.partial(body, None, acc_scr.at[(n_tiles - 1) % 2]))


def _proj_qkv(h, w_in, tables, group, B, S):
    M, K = h.shape
    dil = DILATIONS[group]
    L = S // dil
    tm, tn = 1024, _COL_BLK
    spt = S // tm
    row_tiles = M // tm
    n_tiles = 3 * row_tiles
    mm = lambda t: jnp.minimum(t, n_tiles - 1)
    ep = lambda t: jnp.maximum(t - 1, 0)
    tbl_spec = pl.BlockSpec((dil, tm // dil, HEAD_DIM), lambda t: (0, ep(t) % row_tiles % spt, 0))
    out = pl.pallas_call(
        functools.partial(_proj_qkv_kernel, row_tiles=row_tiles, n_tiles=n_tiles, dil=dil),
        grid=(n_tiles + 1,),
        in_specs=[pl.BlockSpec((tm, K), lambda t: (mm(t) % row_tiles, 0)),
                  pl.BlockSpec((K, tn),
                               lambda t: (0, _Q_BLK0 + group + 3 * (mm(t) // row_tiles))),
                  tbl_spec, tbl_spec, tbl_spec],
        out_specs=pl.BlockSpec(
            (1, 1, HEADS_PER_GROUP, dil, tm // dil, HEAD_DIM),
            lambda t: (ep(t) // row_tiles, ep(t) % row_tiles // spt, 0, 0,
                       ep(t) % row_tiles % spt, 0)),
        out_shape=jax.ShapeDtypeStruct((3, B, HEADS_PER_GROUP, dil, L, HEAD_DIM), BF16),
        scratch_shapes=[pltpu.VMEM((K, tn), BF16),
                        pltpu.VMEM((2, HEADS_PER_GROUP, tm, HEAD_DIM), F32)]
        + ([pltpu.VMEM((HEADS_PER_GROUP, _MAX_ROW_STRIDE, tm // _MAX_ROW_STRIDE, HEAD_DIM), F32)]
           if dil > _MAX_ROW_STRIDE else []),
        compiler_params=_cparams(1),
        name=f"proj_qkv_g{group}",
    )(h, w_in, *[tbl.reshape(dil, L, HEAD_DIM) for tbl in tables])
    return out.reshape(3, B, HEADS_PER_GROUP, S, HEAD_DIM)


def _rope_tables(S, dilation):
    half = ROT_DIM // 2
    inv = np.exp(-math.log(ROPE_THETA) * np.arange(half, dtype=np.float32)
                 * np.float32(2.0 / ROT_DIM)).astype(np.float32)
    L = S // dilation
    p = np.arange(S)
    pos = (p % L) * dilation + p // L
    ang = pos.astype(np.float32)[:, None] * inv[None, :]
    cos, sin = np.cos(ang).astype(np.float32), np.sin(ang).astype(np.float32)
    cos_t = np.ones((S, HEAD_DIM), np.float32)
    sin_a = np.zeros((S, HEAD_DIM), np.float32)
    sin_b = np.zeros((S, HEAD_DIM), np.float32)
    cos_t[:, :half] = cos
    cos_t[:, half:ROT_DIM] = cos
    sin_a[:, :half] = -sin
    sin_b[:, half:ROT_DIM] = sin
    return jnp.asarray(cos_t), jnp.asarray(sin_a), jnp.asarray(sin_b)


_RNN_TS = 256
_PAD = 8


def _softplus(z):
    return jnp.maximum(z, 0.0) + jnp.log1p(jnp.exp(-jnp.abs(z)))


def _scan8(a, b, row):
    for k in (1, 2, 4):
        a_sh = jnp.where(row >= k, pltpu.roll(a, k, 0), 1.0)
        b_sh = jnp.where(row >= k, pltpu.roll(b, k, 0), 0.0)
        b = a * b_sh + b
        a = a * a_sh
    return a, b


_RNN_PHASES = 4


def _rglru_gates(c, x_ref, cw_ref, cb_ref, wrg_ref, wig_ref, xpad, stage):
    ts, nph = _RNN_TS, _RNN_PHASES
    q = ts // nph
    W = RNN_BLOCK_W
    cs = slice(c * W, (c + 1) * W)
    xpad[c, _PAD:_PAD + ts, :] = x_ref[:, cs].astype(F32)
    xs = {off: xpad[c, pl.ds(_PAD + off, q, stride=nph), :]
          for off in range(-(CONV_W - 1), nph)}
    w = [cw_ref[k:k + 1, cs] for k in range(CONV_W)]
    parts = []
    for j in range(nph):
        acc = cb_ref[:, cs] + w[0] * xs[j - (CONV_W - 1)]
        for k in range(1, CONV_W):
            acc = acc + w[k] * xs[j - (CONV_W - 1) + k]
        parts.append(acc)
    xc = jnp.concatenate(parts, axis=0)
    xb = xc.astype(BF16)
    xpad[c, 0:_PAD, :] = xpad[c, ts:ts + _PAD, :]
    stage[0] = xc
    stage[1] = jnp.dot(xb, wrg_ref[c].astype(BF16), preferred_element_type=F32)
    stage[2] = jnp.dot(xb, wig_ref[c].astype(BF16), preferred_element_type=F32)


def _rglru_scan(c, stage, gate_ref, brg_ref, big_ref, lam_ref, o_ref, h_buf, carry):
    ts, nph = _RNN_TS, _RNN_PHASES
    q = ts // nph
    W = RNN_BLOCK_W
    row = lax.broadcasted_iota(jnp.int32, (8, W), 0)
    cs = slice(c * W, (c + 1) * W)
    xc = stage[0]
    r = jax.nn.sigmoid(stage[1] + brg_ref[:, cs])
    ig = jax.nn.sigmoid(stage[2] + big_ref[:, cs])
    log_a = (-LRU_C * _softplus(-lam_ref[:, cs])) * r
    a = jnp.exp(log_a)
    om = 1.0 - a * a
    stage[1] = a
    stage[2] = jnp.where(om > 0.0, om * lax.rsqrt(om), 0.0) * (ig * xc)
    part = lambda k, j: stage[k, j * q:(j + 1) * q, :]

    tot_a, tot_b = part(1, 0), part(2, 0)
    for j in range(1, nph):
        tot_b = part(1, j) * tot_b + part(2, j)
        tot_a = part(1, j) * tot_a
    h_in = carry[c]
    seg_in = []
    for m in range(q // 8):
        pa, pb = _scan8(tot_a[m * 8:(m + 1) * 8], tot_b[m * 8:(m + 1) * 8], row)
        after = pa * h_in + pb
        seg_in.append(jnp.where(row >= 1, pltpu.roll(after, 1, 0), h_in))
        h_in = after[7:8, :]
    carry[c] = h_in
    h = jnp.concatenate(seg_in, axis=0)
    for j in range(nph):
        h = part(1, j) * h + part(2, j)
        h_buf[pl.ds(j, q, stride=nph), :] = h
    g = gate_ref[:, cs].astype(F32)
    o_ref[:, cs] = (h_buf[...] * (g * jax.nn.sigmoid(g))).astype(BF16)


_SCALE = HEAD_DIM ** -0.5
_ATTN_BATCH = 16


def _attend(q, k, v, has_prev):
    n, nk, _ = k.shape
    s = jnp.einsum("nqd,nkd->nqk", q, k, preferred_element_type=F32)
    qi = lax.broadcasted_iota(jnp.int32, (1, ATTN_BLOCK, nk), 1)
    kj = lax.broadcasted_iota(jnp.int32, (1, ATTN_BLOCK, nk), 2)
    if has_prev:
        d = kj - qi
        valid = (d >= 0) & (d <= ATTN_BLOCK)
    else:
        valid = kj <= qi
    s = jnp.where(valid, s, MASK_VALUE)
    m = jnp.max(s, axis=-1, keepdims=True)
    p = jnp.exp2((s - m) * (_SCALE * math.log2(math.e)))
    v_ones = jnp.concatenate([v, jnp.ones_like(v)], axis=-1)
    oe = jnp.einsum("nqk,nkd->nqd", p.astype(BF16), v_ones, preferred_element_type=F32)
    l = oe[:, :, HEAD_DIM:]
    o = oe[:, :, :HEAD_DIM] * (1.0 / l)
    return o, m * _SCALE + jnp.log(l)


def _attn_kernel(g0_ref, g1_ref, g2_ref, gate_ref, out_ref, o_scr, ld_scr):
    S = o_scr.shape[1]
    blk = ATTN_BLOCK

    def run_blocks(ref, bases, has_prev):
        res = []
        for i in range(0, len(bases), _ATTN_BATCH):
            chunk = bases[i:i + _ATTN_BATCH]
            lo = blk if has_prev else 0
            q = jnp.stack([ref[0, 0, 0, b:b + blk, :] for b in chunk])
            k = jnp.stack([ref[1, 0, 0, b - lo:b + blk, :] for b in chunk])
            v = jnp.stack([ref[2, 0, 0, b - lo:b + blk, :] for b in chunk])
            o, ld = _attend(q, k, v, has_prev)
            res += [(o[j], ld[j]) for j in range(len(chunk))]
        return res

    for gi, (ref, dil) in enumerate(zip((g0_ref, g1_ref, g2_ref), DILATIONS)):
        L = S // dil
        for has_prev in (False, True):
            blocks = [(r, n) for r in range(dil) for n in range(L // blk) if (n > 0) == has_prev]
            if not blocks:
                continue
            outs = run_blocks(ref, [r * L + n * blk for r, n in blocks], has_prev)
            for (r, n), (o, ld) in zip(blocks, outs):
                if dil == 1:
                    rows = pl.ds(n * blk, blk)
                else:
                    rows = pl.ds(n * blk * dil + r, blk, stride=dil)
                o_scr[gi, rows, :] = o
                ld_scr[gi, rows, :] = ld

    ch = 256

    def merge(c, _):
        rows = pl.ds(pl.multiple_of(c * ch, ch), ch)
        l0, l1, l2 = ld_scr[0, rows, :], ld_scr[1, rows, :], ld_scr[2, rows, :]
        m = jnp.maximum(jnp.maximum(l0, l1), l2)
        e0, e1, e2 = jnp.exp(l0 - m), jnp.exp(l1 - m), jnp.exp(l2 - m)
        num = e0 * o_scr[0, rows, :] + e1 * o_scr[1, rows, :] + e2 * o_scr[2, rows, :]
        o = num * (1.0 / (e0 + e1 + e2))
        g = gate_ref[0, rows, :].astype(F32)
        out_ref[0, rows, :] = (o * (g * jax.nn.sigmoid(g))).astype(BF16)
        return 0

    lax.fori_loop(0, S // ch, merge, 0)


def _attention(qkv, p_nat, B, S):
    H = HEADS_PER_GROUP
    qkv_spec = lambda: pl.BlockSpec((3, 1, 1, S, HEAD_DIM), lambda b, h: (0, b, h, 0, 0))
    return pl.pallas_call(
        _attn_kernel,
        grid=(B, H),
        in_specs=[qkv_spec(), qkv_spec(), qkv_spec(),
                  pl.BlockSpec((1, S, HEAD_DIM),
                               lambda b, h: (b, 0, _P_GATE_BLK * _COL_BLK // HEAD_DIM + h))],
        out_specs=pl.BlockSpec((1, S, HEAD_DIM), lambda b, h: (b, 0, h)),
        out_shape=jax.ShapeDtypeStruct((B, S, H * HEAD_DIM), BF16),
        scratch_shapes=[pltpu.VMEM((3, S, HEAD_DIM), F32), pltpu.VMEM((3, S, HEAD_DIM), F32)],
        compiler_params=_cparams(2),
        name="dilated_attn",
    )(*qkv, p_nat)


_DOT_LANES = 256


def _tail_kernel(xr_ref, gr_ref, cw_ref, cb_ref, wrg_ref, brg_ref, wig_ref, big_ref, lam_ref,
                 u_attn_ref, g_rnn_ref, g_attn_ref, x_ref, w_rnn_ref, w_attn_ref, w_o_ref,
                 g_post_ref, o_ref, xpad, stage, h_scr, carry, u_scr, merged_scr, y_scr, *,
                 tiles_per_seq):
    s = pl.program_id(0)

    @pl.when(s == 0)
    def _():
        u_scr[...] = jnp.zeros(u_scr.shape, BF16)

    @pl.when(s % tiles_per_seq == 0)
    def _():
        xpad[:, 0:_PAD, :] = jnp.zeros((RNN_BLOCKS, _PAD, RNN_BLOCK_W), F32)
        carry[...] = jnp.zeros(carry.shape, F32)

    def body(u_next, u_cur):
        n_chunks = D_MODEL // _DOT_LANES
        gates = lambda c: _rglru_gates(c, xr_ref, cw_ref, cb_ref, wrg_ref, wig_ref, xpad,
                                       stage.at[c % 2])
        gates(0)
        for c in range(RNN_BLOCKS):
            if c + 1 < RNN_BLOCKS:
                gates(c + 1)
            _rglru_scan(c, stage.at[c % 2], gr_ref, brg_ref, big_ref, lam_ref, u_next,
                        h_scr.at[c % 2], carry)
            lanes = slice((c % n_chunks) * _DOT_LANES, (c % n_chunks + 1) * _DOT_LANES)
            if c < n_chunks:
                y_rnn = jnp.dot(u_cur[...], w_rnn_ref[:, lanes], preferred_element_type=F32)
                y_attn = jnp.dot(u_attn_ref[...], w_attn_ref[:, lanes],
                                 preferred_element_type=F32)
                merged_scr[:, lanes] = (
                    jax.nn.sigmoid(g_rnn_ref[:, lanes].astype(F32)) * y_rnn
                    + jax.nn.sigmoid(g_attn_ref[:, lanes].astype(F32)) * y_attn).astype(BF16)
            else:
                y_scr[:, lanes] = jnp.dot(merged_scr[...], w_o_ref[:, lanes],
                                          preferred_element_type=F32)
        y = y_scr[...]
        ms = jnp.mean(y * y, axis=-1, keepdims=True)
        o_ref[...] = x_ref[...] + y * lax.rsqrt(ms + NORM_EPS) * g_post_ref[...]

    for parity in range(2):
        pl.when(s % 2 == parity)(
            functools.partial(body, u_scr.at[parity], u_scr.at[1 - parity]))


def _tail(p_x, p_nat, u_attn, x2d, conv_w, conv_b, w_rg, b_rg, w_ig, b_ig, lam,
          w_rnn_b, w_attn_b, w_o_b, g_post, S):
    M, D = x2d.shape
    tm, C, W = _RNN_TS, D_RNN, RNN_BLOCK_W
    n_tiles = M // tm
    assert RNN_BLOCKS == 2 * (D // _DOT_LANES)
    nxt = lambda s: jnp.minimum(s, n_tiles - 1)
    cur = lambda s: jnp.maximum(s - 1, 0)
    const = lambda shape: pl.BlockSpec(shape, lambda s: (0,) * len(shape))
    resident = lambda shape: pl.BlockSpec(shape, lambda s: (0, 0), pipeline_mode=pl.Buffered(1))
    return pl.pallas_call(
        functools.partial(_tail_kernel, tiles_per_seq=S // tm),
        grid=(n_tiles + 1,),
        in_specs=[pl.BlockSpec((tm, C), lambda s: (nxt(s), 0)),
                  pl.BlockSpec((tm, C), lambda s: (nxt(s), _P_RGATE_BLK * _COL_BLK // C)),
                  const((CONV_W, C)), const((1, C)), const((RNN_BLOCKS, W, W)), const((1, C)),
                  const((RNN_BLOCKS, W, W)), const((1, C)), const((1, C)),
                  pl.BlockSpec((tm, u_attn.shape[1]), lambda s: (cur(s), 0)),
                  pl.BlockSpec((tm, D), lambda s: (cur(s), _P_GRNN_BLK * _COL_BLK // D)),
                  pl.BlockSpec((tm, D), lambda s: (cur(s), _P_GATTN_BLK * _COL_BLK // D)),
                  pl.BlockSpec((tm, D), lambda s: (cur(s), 0)),
                  resident(w_rnn_b.shape), resident(w_attn_b.shape), resident(w_o_b.shape),
                  const((1, D))],
        out_specs=pl.BlockSpec((tm, D), lambda s: (cur(s), 0)),
        out_shape=jax.ShapeDtypeStruct((M, D), F32),
        scratch_shapes=[pltpu.VMEM((RNN_BLOCKS, tm + _PAD, W), F32),
                        pltpu.VMEM((2, 3, tm, W), F32),
                        pltpu.VMEM((2, tm, W), F32),
                        pltpu.VMEM((RNN_BLOCKS, 1, W), F32),
                        pltpu.VMEM((2, tm, C), BF16),
                        pltpu.VMEM((tm, D), BF16),
                        pltpu.VMEM((tm, D), F32)],
        compiler_params=_cparams(1),
        name="rglru_output",
    )(p_x, p_nat, conv_w, conv_b.reshape(1, C), w_rg, b_rg.reshape(1, C), w_ig,
      b_ig.reshape(1, C), lam.reshape(1, C), u_attn, p_nat, p_nat, x2d,
      w_rnn_b, w_attn_b, w_o_b, g_post.reshape(1, D))


def _layer(x, ln_pre_g, w_in, conv_w, conv_b, w_rg, b_rg, w_ig, b_ig, lru_lambda,
           w_rnn_out, w_attn_out, w_o, ln_post_g):
    B, S, D = x.shape
    M = B * S
    x2d = x.reshape(M, D)
    h, p_x = _prenorm(x2d, ln_pre_g, w_in)
    p_nat, w_rnn_b, w_attn_b, w_o_b = _proj_nat(h, w_in, (w_rnn_out, w_attn_out, w_o))
    qkv = [_proj_qkv(h, w_in, _rope_tables(S, dil), g, B, S) for g, dil in enumerate(DILATIONS)]
    u_attn = _attention(qkv, p_nat.reshape(B, S, p_nat.shape[1]), B, S)
    out = _tail(p_x, p_nat, u_attn.reshape(M, -1), x2d, conv_w, conv_b, w_rg, b_rg, w_ig, b_ig,
                lru_lambda, w_rnn_b, w_attn_b, w_o_b, ln_post_g, S)
    return out.reshape(B, S, D)


def kernel(x, ln_pre_g, w_in, conv_w, conv_b, w_rg, b_rg, w_ig, b_ig, lru_lambda,
           w_rnn_out, w_attn_out, w_o, ln_post_g):
    for layer in range(ln_pre_g.shape[0]):
        x = _layer(x, ln_pre_g[layer], w_in[layer], conv_w[layer], conv_b[layer], w_rg[layer],
                   b_rg[layer], w_ig[layer], b_ig[layer], lru_lambda[layer], w_rnn_out[layer],
                   w_attn_out[layer], w_o[layer], ln_post_g[layer])
    return x
```

```python
import functools
import math

import jax
import jax.numpy as jnp
import numpy as np
from jax import lax
from jax.experimental import pallas as pl
from jax.experimental.pallas import tpu as pltpu

D_MODEL = 2048
D_RNN = 2048
RNN_BLOCKS = 16
RNN_BLOCK_W = 128
CONV_W = 4
LRU_C = 8.0
HEAD_DIM = 128
HEADS_PER_GROUP = 8
DILATIONS = (1, 4, 16)
ATTN_BLOCK = 128
ROPE_THETA = 500000.0
ROT_DIM = 32
NORM_EPS = 1e-6
MASK_VALUE = -1e30

BF16 = jnp.bfloat16
F32 = jnp.float32

_COL_BLK = 1024
_Q_BLK0 = 4
_NAT_BLOCKS = 7
_P_RGATE_BLK = 0
_P_GRNN_BLK = 2
_P_GATTN_BLK = 4
_P_GATE_BLK = 6


def _nat_w_block(j):
    return jnp.where(j < 2, j + 2, jnp.where(j < 6, j + 12, 13))

_HEADS_PER_DOT = 2

_VMEM_LIMIT = 56 * 1024 * 1024


def _cparams(n_axes):
    return pltpu.CompilerParams(
        dimension_semantics=("arbitrary",) * n_axes, vmem_limit_bytes=_VMEM_LIMIT)


_NORM_T = 512


def _prenorm_kernel(x_ref, g_ref, w_ref, h_ref, px_ref, wb_ref):
    @pl.when(pl.program_id(0) == 0)
    def _():
        wb_ref[...] = w_ref[...].astype(BF16)

    x = x_ref[...]
    ms = jnp.mean(x * x, axis=-1, keepdims=True)
    h = (x * lax.rsqrt(ms + NORM_EPS) * g_ref[...]).astype(BF16)
    h_ref[...] = h
    px_ref[...] = jnp.dot(h, wb_ref[...], preferred_element_type=F32).astype(BF16)


def _prenorm(x2d, g, w_in):
    M, D = x2d.shape
    T = _NORM_T
    return pl.pallas_call(
        _prenorm_kernel,
        grid=(M // T,),
        in_specs=[pl.BlockSpec((T, D), lambda i: (i, 0)),
                  pl.BlockSpec((1, D), lambda i: (0, 0)),
                  pl.BlockSpec((D, D_RNN), lambda i: (0, 0), pipeline_mode=pl.Buffered(1))],
        out_specs=[pl.BlockSpec((T, D), lambda i: (i, 0)),
                   pl.BlockSpec((T, D_RNN), lambda i: (i, 0))],
        out_shape=[jax.ShapeDtypeStruct((M, D), BF16), jax.ShapeDtypeStruct((M, D_RNN), BF16)],
        scratch_shapes=[pltpu.VMEM((D, D_RNN), BF16)],
        compiler_params=_cparams(1),
        name="prenorm_proj_x",
    )(x2d, g.reshape(1, D), w_in)


def _cast_weight_once(w_ref, wb_ref):
    @pl.when(pl.program_id(1) == 0)
    def _():
        wb_ref[...] = w_ref[...].astype(BF16)


_CAST_ROWS = 256


def _proj_nat_kernel(lhs_ref, w_ref, *rest, cast_steps):
    n = len(cast_steps)
    wf_refs, o_ref, wo_refs, wb_ref = rest[:n], rest[n], rest[n + 1:2 * n + 1], rest[2 * n + 1]
    _cast_weight_once(w_ref, wb_ref)
    step = pl.program_id(0) * pl.num_programs(1) + pl.program_id(1)
    for wf_ref, wo_ref, steps in zip(wf_refs, wo_refs, cast_steps):
        @pl.when(step < steps)
        def _(wf_ref=wf_ref, wo_ref=wo_ref):
            wo_ref[...] = wf_ref[...].astype(BF16)
    o_ref[...] = jnp.dot(lhs_ref[...], wb_ref[...], preferred_element_type=F32).astype(BF16)


def _proj_nat(h, w_in, out_weights):
    M, K = h.shape
    tm, tn = 1024, _COL_BLK
    row_tiles = M // tm
    cast_steps = tuple(w.shape[0] // _CAST_ROWS for w in out_weights)
    assert max(cast_steps) <= _NAT_BLOCKS * row_tiles

    def chunk_spec(w, steps):
        return pl.BlockSpec((_CAST_ROWS, w.shape[1]),
                            lambda j, i: (jnp.minimum(j * row_tiles + i, steps - 1), 0))

    return pl.pallas_call(
        functools.partial(_proj_nat_kernel, cast_steps=cast_steps),
        grid=(_NAT_BLOCKS, row_tiles),
        in_specs=[pl.BlockSpec((tm, K), lambda j, i: (i, 0)),
                  pl.BlockSpec((K, tn), lambda j, i: (0, _nat_w_block(j)))]
        + [chunk_spec(w, n) for w, n in zip(out_weights, cast_steps)],
        out_specs=[pl.BlockSpec((tm, tn), lambda j, i: (i, j))]
        + [chunk_spec(w, n) for w, n in zip(out_weights, cast_steps)],
        out_shape=[jax.ShapeDtypeStruct((M, _NAT_BLOCKS * tn), BF16)]
        + [jax.ShapeDtypeStruct(w.shape, BF16) for w in out_weights],
        scratch_shapes=[pltpu.VMEM((K, tn), BF16)],
        compiler_params=_cparams(2),
        name="proj_nat",
    )(h, w_in, *out_weights)


_MAX_ROW_STRIDE = 4


def _proj_qkv_kernel(lhs_ref, w_ref, cos_ref, sa_ref, sb_ref, o_ref, wb_ref, acc_scr, *rest,
                     row_tiles, n_tiles, dil):
    t = pl.program_id(0)
    tm = acc_scr.shape[2]
    rows = tm // dil

    def class_rows(acc_prev, c):
        if dil == 1:
            return {0: acc_prev[c]}
        if dil <= _MAX_ROW_STRIDE:
            return {r: acc_prev[c, pl.ds(r, rows, stride=dil), :] for r in range(dil)}
        (tmp_scr,) = rest
        s0 = _MAX_ROW_STRIDE
        s1 = dil // s0
        assert s1 <= _MAX_ROW_STRIDE
        for r0 in range(s0):
            tmp_scr[c, r0] = acc_prev[c, pl.ds(r0, tm // s0, stride=s0), :]
        return {s0 * r1 + r0: tmp_scr[c, r0, pl.ds(r1, rows, stride=s1), :]
                for r0 in range(s0) for r1 in range(s1)}

    @pl.when(t == 0)
    def _():
        acc_scr[...] = jnp.zeros(acc_scr.shape, F32)

    @pl.when((t % row_tiles == 0) & (t < n_tiles))
    def _():
        wb_ref[...] = w_ref[...].astype(BF16)

    def body(acc_cur, acc_prev):
        is_qk = (t - 1) // row_tiles < 2
        cos = jnp.where(is_qk, cos_ref[...], 1.0)
        sa = jnp.where(is_qk, sa_ref[...], 0.0)
        sb = jnp.where(is_qk, sb_ref[...], 0.0)
        for c0 in range(0, HEADS_PER_GROUP, _HEADS_PER_DOT):
            lanes = slice(c0 * HEAD_DIM, (c0 + _HEADS_PER_DOT) * HEAD_DIM)
            if acc_cur is not None:
                acc = jnp.dot(lhs_ref[...], wb_ref[:, lanes], preferred_element_type=F32)
                for c in range(c0, c0 + _HEADS_PER_DOT):
                    acc_cur[c] = acc[:, (c - c0) * HEAD_DIM:(c - c0 + 1) * HEAD_DIM]
            for c in range(c0, c0 + _HEADS_PER_DOT):
                for r, v in class_rows(acc_prev, c).items():
                    rot = (v * cos[r] + pltpu.roll(v, HEAD_DIM - ROT_DIM // 2, 1) * sa[r]
                           + pltpu.roll(v, ROT_DIM // 2, 1) * sb[r])
                    o_ref[0, 0, c, r] = rot.astype(BF16)

    for parity in range(2):
        pl.when((t % 2 == parity) & (t < n_tiles))(
            functools.partial(body, acc_scr.at[parity], acc_scr.at[1 - parity]))
    pl.when(t == n_tiles)(functools.partial(body, None, acc_scr.at[(n_tiles - 1) % 2]))


def _proj_qkv(h, w_in, tables, group, B, S):
    M, K = h.shape
    dil = DILATIONS[group]
    L = S // dil
    tm, tn = 1024, _COL_BLK
    spt = S // tm
    row_tiles = M // tm
    n_tiles = 3 * row_tiles
    mm = lambda t: jnp.minimum(t, n_tiles - 1)
    ep = lambda t: jnp.maximum(t - 1, 0)
    tbl_spec = pl.BlockSpec((dil, tm // dil, HEAD_DIM), lambda t: (0, ep(t) % row_tiles % spt, 0))
    out = pl.pallas_call(
        functools.partial(_proj_qkv_kernel, row_tiles=row_tiles, n_tiles=n_tiles, dil=dil),
        grid=(n_tiles + 1,),
        in_specs=[pl.BlockSpec((tm, K), lambda t: (mm(t) % row_tiles, 0)),
                  pl.BlockSpec((K, tn),
                               lambda t: (0, _Q_BLK0 + group + 3 * (mm(t) // row_tiles))),
                  tbl_spec, tbl_spec, tbl_spec],
        out_specs=pl.BlockSpec(
            (1, 1, HEADS_PER_GROUP, dil, tm // dil, HEAD_DIM),
            lambda t: (ep(t) // row_tiles, ep(t) % row_tiles // spt, 0, 0,
                       ep(t) % row_tiles % spt, 0)),
        out_shape=jax.ShapeDtypeStruct((3, B, HEADS_PER_GROUP, dil, L, HEAD_DIM), BF16),
        scratch_shapes=[pltpu.VMEM((K, tn), BF16),
                        pltpu.VMEM((2, HEADS_PER_GROUP, tm, HEAD_DIM), F32)]
        + ([pltpu.VMEM((HEADS_PER_GROUP, _MAX_ROW_STRIDE, tm // _MAX_ROW_STRIDE, HEAD_DIM), F32)]
           if dil > _MAX_ROW_STRIDE else []),
        compiler_params=_cparams(1),
        name=f"proj_qkv_g{group}",
    )(h, w_in, *[tbl.reshape(dil, L, HEAD_DIM) for tbl in tables])
    return out.reshape(3, B, HEADS_PER_GROUP, S, HEAD_DIM)


def _rope_tables(S, dilation):
    half = ROT_DIM // 2
    inv = np.exp(-math.log(ROPE_THETA) * np.arange(half, dtype=np.float32)
                 * np.float32(2.0 / ROT_DIM)).astype(np.float32)
    L = S // dilation
    p = np.arange(S)
    pos = (p % L) * dilation + p // L
    ang = pos.astype(np.float32)[:, None] * inv[None, :]
    cos, sin = np.cos(ang).astype(np.float32), np.sin(ang).astype(np.float32)
    cos_t = np.ones((S, HEAD_DIM), np.float32)
    sin_a = np.zeros((S, HEAD_DIM), np.float32)
    sin_b = np.zeros((S, HEAD_DIM), np.float32)
    cos_t[:, :half] = cos
    cos_t[:, half:ROT_DIM] = cos
    sin_a[:, :half] = -sin
    sin_b[:, half:ROT_DIM] = sin
    return jnp.asarray(cos_t), jnp.asarray(sin_a), jnp.asarray(sin_b)


_RNN_TS = 256
_PAD = 8


def _softplus(z):
    return jnp.maximum(z, 0.0) + jnp.log1p(jnp.exp(-jnp.abs(z)))


def _scan8(a, b, row):
    for k in (1, 2, 4):
        a_sh = jnp.where(row >= k, pltpu.roll(a, k, 0), 1.0)
        b_sh = jnp.where(row >= k, pltpu.roll(b, k, 0), 0.0)
        b = a * b_sh + b
        a = a * a_sh
    return a, b


_RNN_PHASES = 4


def _rglru_gates(c, x_ref, cw_ref, cb_ref, wrg_ref, wig_ref, xpad, stage):
    ts, nph = _RNN_TS, _RNN_PHASES
    q = ts // nph
    W = RNN_BLOCK_W
    cs = slice(c * W, (c + 1) * W)
    xpad[c, _PAD:_PAD + ts, :] = x_ref[:, cs].astype(F32)
    xs = {off: xpad[c, pl.ds(_PAD + off, q, stride=nph), :]
          for off in range(-(CONV_W - 1), nph)}
    w = [cw_ref[k:k + 1, cs] for k in range(CONV_W)]
    parts = []
    for j in range(nph):
        acc = cb_ref[:, cs] + w[0] * xs[j - (CONV_W - 1)]
        for k in range(1, CONV_W):
            acc = acc + w[k] * xs[j - (CONV_W - 1) + k]
        parts.append(acc)
    xc = jnp.concatenate(parts, axis=0)
    xb = xc.astype(BF16)
    xpad[c, 0:_PAD, :] = xpad[c, ts:ts + _PAD, :]
    stage[0] = xc
    stage[1] = jnp.dot(xb, wrg_ref[c].astype(BF16), preferred_element_type=F32)
    stage[2] = jnp.dot(xb, wig_ref[c].astype(BF16), preferred_element_type=F32)


def _rglru_scan(c, stage, gate_ref, brg_ref, big_ref, lam_ref, o_ref, h_buf, carry):
    ts, nph = _RNN_TS, _RNN_PHASES
    q = ts // nph
    W = RNN_BLOCK_W
    row = lax.broadcasted_iota(jnp.int32, (8, W), 0)
    cs = slice(c * W, (c + 1) * W)
    xc = stage[0]
    r = jax.nn.sigmoid(stage[1] + brg_ref[:, cs])
    ig = jax.nn.sigmoid(stage[2] + big_ref[:, cs])
    log_a = (-LRU_C * _softplus(-lam_ref[:, cs])) * r
    a = jnp.exp(log_a)
    om = 1.0 - a * a
    stage[1] = a
    stage[2] = jnp.where(om > 0.0, om * lax.rsqrt(om), 0.0) * (ig * xc)
    part = lambda k, j: stage[k, j * q:(j + 1) * q, :]

    tot_a, tot_b = part(1, 0), part(2, 0)
    for j in range(1, nph):
        tot_b = part(1, j) * tot_b + part(2, j)
        tot_a = part(1, j) * tot_a
    h_in = carry[c]
    seg_in = []
    for m in range(q // 8):
        pa, pb = _scan8(tot_a[m * 8:(m + 1) * 8], tot_b[m * 8:(m + 1) * 8], row)
        after = pa * h_in + pb
        seg_in.append(jnp.where(row >= 1, pltpu.roll(after, 1, 0), h_in))
        h_in = after[7:8, :]
    carry[c] = h_in
    h = jnp.concatenate(seg_in, axis=0)
    for j in range(nph):
        h = part(1, j) * h + part(2, j)
        h_buf[pl.ds(j, q, stride=nph), :] = h
    g = gate_ref[:, cs].astype(F32)
    o_ref[:, cs] = (h_buf[...] * (g * jax.nn.sigmoid(g))).astype(BF16)


_SCALE = HEAD_DIM ** -0.5
_ATTN_BATCH = 16


def _attend(q, k, v, has_prev):
    n, nk, _ = k.shape
    s = jnp.einsum("nqd,nkd->nqk", q, k, preferred_element_type=F32)
    qi = lax.broadcasted_iota(jnp.int32, (1, ATTN_BLOCK, nk), 1)
    kj = lax.broadcasted_iota(jnp.int32, (1, ATTN_BLOCK, nk), 2)
    if has_prev:
        d = kj - qi
        valid = (d >= 0) & (d <= ATTN_BLOCK)
    else:
        valid = kj <= qi
    s = jnp.where(valid, s, MASK_VALUE)
    m = jnp.max(s, axis=-1, keepdims=True)
    p = jnp.exp2((s - m) * (_SCALE * math.log2(math.e)))
    v_ones = jnp.concatenate([v, jnp.ones_like(v)], axis=-1)
    oe = jnp.einsum("nqk,nkd->nqd", p.astype(BF16), v_ones, preferred_element_type=F32)
    l = oe[:, :, HEAD_DIM:]
    o = oe[:, :, :HEAD_DIM] * (1.0 / l)
    return o, m * _SCALE + jnp.log(l)


def _attn_kernel(g0_ref, g1_ref, g2_ref, gate_ref, out_ref, o_scr, ld_scr):
    S = o_scr.shape[1]
    blk = ATTN_BLOCK

    def run_blocks(ref, bases, has_prev):
        res = []
        for i in range(0, len(bases), _ATTN_BATCH):
            chunk = bases[i:i + _ATTN_BATCH]
            lo = blk if has_prev else 0
            q = jnp.stack([ref[0, 0, 0, b:b + blk, :] for b in chunk])
            k = jnp.stack([ref[1, 0, 0, b - lo:b + blk, :] for b in chunk])
            v = jnp.stack([ref[2, 0, 0, b - lo:b + blk, :] for b in chunk])
            o, ld = _attend(q, k, v, has_prev)
            res += [(o[j], ld[j]) for j in range(len(chunk))]
        return res

    for gi, (ref, dil) in enumerate(zip((g0_ref, g1_ref, g2_ref), DILATIONS)):
        L = S // dil
        for has_prev in (False, True):
            blocks = [(r, n) for r in range(dil) for n in range(L // blk) if (n > 0) == has_prev]
            if not blocks:
                continue
            outs = run_blocks(ref, [r * L + n * blk for r, n in blocks], has_prev)
            for (r, n), (o, ld) in zip(blocks, outs):
                if dil == 1:
                    rows = pl.ds(n * blk, blk)
                else:
                    rows = pl.ds(n * blk * dil + r, blk, stride=dil)
                o_scr[gi, rows, :] = o
                ld_scr[gi, rows, :] = ld

    ch = 256

    def merge(c, _):
        rows = pl.ds(pl.multiple_of(c * ch, ch), ch)
        l0, l1, l2 = ld_scr[0, rows, :], ld_scr[1, rows, :], ld_scr[2, rows, :]
        m = jnp.maximum(jnp.maximum(l0, l1), l2)
        e0, e1, e2 = jnp.exp(l0 - m), jnp.exp(l1 - m), jnp.exp(l2 - m)
        num = e0 * o_scr[0, rows, :] + e1 * o_scr[1, rows, :] + e2 * o_scr[2, rows, :]
        o = num * (1.0 / (e0 + e1 + e2))
        g = gate_ref[0, rows, :].astype(F32)
        out_ref[0, rows, :] = (o * (g * jax.nn.sigmoid(g))).astype(BF16)
        return 0

    lax.fori_loop(0, S // ch, merge, 0)


def _attention(qkv, p_nat, B, S):
    H = HEADS_PER_GROUP
    qkv_spec = lambda: pl.BlockSpec((3, 1, 1, S, HEAD_DIM), lambda b, h: (0, b, h, 0, 0))
    return pl.pallas_call(
        _attn_kernel,
        grid=(B, H),
        in_specs=[qkv_spec(), qkv_spec(), qkv_spec(),
                  pl.BlockSpec((1, S, HEAD_DIM),
                               lambda b, h: (b, 0, _P_GATE_BLK * _COL_BLK // HEAD_DIM + h))],
        out_specs=pl.BlockSpec((1, S, HEAD_DIM), lambda b, h: (b, 0, h)),
        out_shape=jax.ShapeDtypeStruct((B, S, H * HEAD_DIM), BF16),
        scratch_shapes=[pltpu.VMEM((3, S, HEAD_DIM), F32), pltpu.VMEM((3, S, HEAD_DIM), F32)],
        compiler_params=_cparams(2),
        name="dilated_attn",
    )(*qkv, p_nat)


_DOT_LANES = 256


def _tail_kernel(xr_ref, gr_ref, cw_ref, cb_ref, wrg_ref, brg_ref, wig_ref, big_ref, lam_ref,
                 u_attn_ref, g_rnn_ref, g_attn_ref, x_ref, w_rnn_ref, w_attn_ref, w_o_ref,
                 g_post_ref, o_ref, xpad, stage, h_scr, carry, u_scr, merged_scr, y_scr, *,
                 tiles_per_seq, n_tiles):
    s = pl.program_id(0)

    @pl.when((s % tiles_per_seq == 0) & (s < n_tiles))
    def _():
        xpad[:, 0:_PAD, :] = jnp.zeros((RNN_BLOCKS, _PAD, RNN_BLOCK_W), F32)
        carry[...] = jnp.zeros(carry.shape, F32)

    def body(u_next, u_cur):
        n_chunks = D_MODEL // _DOT_LANES
        gates = lambda c: _rglru_gates(c, xr_ref, cw_ref, cb_ref, wrg_ref, wig_ref, xpad,
                                       stage.at[c % 2])
        if u_next is not None:
            gates(0)
        for c in range(RNN_BLOCKS):
            if u_next is not None:
                if c + 1 < RNN_BLOCKS:
                    gates(c + 1)
                _rglru_scan(c, stage.at[c % 2], gr_ref, brg_ref, big_ref, lam_ref, u_next,
                            h_scr.at[c % 2], carry)
            if u_cur is None:
                continue
            lanes = slice((c % n_chunks) * _DOT_LANES, (c % n_chunks + 1) * _DOT_LANES)
            if c < n_chunks:
                y_rnn = jnp.dot(u_cur[...], w_rnn_ref[:, lanes], preferred_element_type=F32)
                y_attn = jnp.dot(u_attn_ref[...], w_attn_ref[:, lanes],
                                 preferred_element_type=F32)
                merged_scr[:, lanes] = (
                    jax.nn.sigmoid(g_rnn_ref[:, lanes].astype(F32)) * y_rnn
                    + jax.nn.sigmoid(g_attn_ref[:, lanes].astype(F32)) * y_attn).astype(BF16)
            else:
                y_scr[:, lanes] = jnp.dot(merged_scr[...], w_o_ref[:, lanes],
                                          preferred_element_type=F32)
        if u_cur is None:
            return
        y = y_scr[...]
        ms = jnp.mean(y * y, axis=-1, keepdims=True)
        o_ref[...] = x_ref[...] + y * lax.rsqrt(ms + NORM_EPS) * g_post_ref[...]

    for parity in range(2):
        pl.when((s % 2 == parity) & (s > 0) & (s < n_tiles))(
            functools.partial(body, u_scr.at[parity], u_scr.at[1 - parity]))
    pl.when(s == 0)(functools.partial(body, u_scr.at[0], None))
    pl.when(s == n_tiles)(functools.partial(body, None, u_scr.at[(n_tiles - 1) % 2]))


def _tail(p_x, p_nat, u_attn, x2d, conv_w, conv_b, w_rg, b_rg, w_ig, b_ig, lam,
          w_rnn_b, w_attn_b, w_o_b, g_post, S):
    M, D = x2d.shape
    tm, C, W = _RNN_TS, D_RNN, RNN_BLOCK_W
    n_tiles = M // tm
    assert RNN_BLOCKS == 2 * (D // _DOT_LANES)
    nxt = lambda s: jnp.minimum(s, n_tiles - 1)
    cur = lambda s: jnp.maximum(s - 1, 0)
    const = lambda shape: pl.BlockSpec(shape, lambda s: (0,) * len(shape))
    resident = lambda shape: pl.BlockSpec(shape, lambda s: (0, 0), pipeline_mode=pl.Buffered(1))
    return pl.pallas_call(
        functools.partial(_tail_kernel, tiles_per_seq=S // tm, n_tiles=n_tiles),
        grid=(n_tiles + 1,),
        in_specs=[pl.BlockSpec((tm, C), lambda s: (nxt(s), 0)),
                  pl.BlockSpec((tm, C), lambda s: (nxt(s), _P_RGATE_BLK * _COL_BLK // C)),
                  const((CONV_W, C)), const((1, C)), const((RNN_BLOCKS, W, W)), const((1, C)),
                  const((RNN_BLOCKS, W, W)), const((1, C)), const((1, C)),
                  pl.BlockSpec((tm, u_attn.shape[1]), lambda s: (cur(s), 0)),
                  pl.BlockSpec((tm, D), lambda s: (cur(s), _P_GRNN_BLK * _COL_BLK // D)),
                  pl.BlockSpec((tm, D), lambda s: (cur(s), _P_GATTN_BLK * _COL_BLK // D)),
                  pl.BlockSpec((tm, D), lambda s: (cur(s), 0)),
                  resident(w_rnn_b.shape), resident(w_attn_b.shape), resident(w_o_b.shape),
                  const((1, D))],
        out_specs=pl.BlockSpec((tm, D), lambda s: (cur(s), 0)),
        out_shape=jax.ShapeDtypeStruct((M, D), F32),
        scratch_shapes=[pltpu.VMEM((RNN_BLOCKS, tm + _PAD, W), F32),
                        pltpu.VMEM((2, 3, tm, W), F32),
                        pltpu.VMEM((2, tm, W), F32),
                        pltpu.VMEM((RNN_BLOCKS, 1, W), F32),
                        pltpu.VMEM((2, tm, C), BF16),
                        pltpu.VMEM((tm, D), BF16),
                        pltpu.VMEM((tm, D), F32)],
        compiler_params=_cparams(1),
        name="rglru_output",
    )(p_x, p_nat, conv_w, conv_b.reshape(1, C), w_rg, b_rg.reshape(1, C), w_ig,
      b_ig.reshape(1, C), lam.reshape(1, C), u_attn, p_nat, p_nat, x2d,
      w_rnn_b, w_attn_b, w_o_b, g_post.reshape(1, D))


def _layer(x, ln_pre_g, w_in, conv_w, conv_b, w_rg, b_rg, w_ig, b_ig, lru_lambda,
           w_rnn_out, w_attn_out, w_o, ln_post_g):
    B, S, D = x.shape
    M = B * S
    x2d = x.reshape(M, D)
    h, p_x = _prenorm(x2d, ln_pre_g, w_in)
    p_nat, w_rnn_b, w_attn_b, w_o_b = _proj_nat(h, w_in, (w_rnn_out, w_attn_out, w_o))
    qkv = [_proj_qkv(h, w_in, _rope_tables(S, dil), g, B, S) for g, dil in enumerate(DILATIONS)]
    u_attn = _attention(qkv, p_nat.reshape(B, S, p_nat.shape[1]), B, S)
    out = _tail(p_x, p_nat, u_attn.reshape(M, -1), x2d, conv_w, conv_b, w_rg, b_rg, w_ig, b_ig,
                lru_lambda, w_rnn_b, w_attn_b, w_o_b, ln_post_g, S)
    return out.reshape(B, S, D)


def kernel(x, ln_pre_g, w_in, conv_w, conv_b, w_rg, b_rg, w_ig, b_ig, lru_lambda,
           w_rnn_out, w_attn_out, w_o, ln_post_g):
    for layer in range(ln_pre_g.shape[0]):
        x = _layer(x, ln_pre_g[layer], w_in[layer], conv_w[layer], conv_b[layer], w_rg[layer],
                   b_rg[layer], w_ig[layer], b_ig[layer], lru_lambda[layer], w_rnn_out[layer],
                   w_attn_out[layer], w_o[layer], ln_post_g[layer])
    return x
```

```python
import functools
import math

import jax
import jax.numpy as jnp
import numpy as np
from jax import lax
from jax.experimental import pallas as pl
from jax.experimental.pallas import tpu as pltpu

D_MODEL = 2048
D_RNN = 2048
RNN_BLOCKS = 16
RNN_BLOCK_W = 128
CONV_W = 4
LRU_C = 8.0
HEAD_DIM = 128
HEADS_PER_GROUP = 8
DILATIONS = (1, 4, 16)
ATTN_BLOCK = 128
ROPE_THETA = 500000.0
ROT_DIM = 32
NORM_EPS = 1e-6
MASK_VALUE = -1e30

BF16 = jnp.bfloat16
F32 = jnp.float32

_COL_BLK = 1024
_Q_BLK0 = 4
_NAT_BLOCKS = 7
_P_RGATE_BLK = 0
_P_GRNN_BLK = 2
_P_GATTN_BLK = 4
_P_GATE_BLK = 6


def _nat_w_block(j):
    return jnp.where(j < 2, j + 2, jnp.where(j < 6, j + 12, 13))

_HEADS_PER_DOT = 4

_VMEM_LIMIT = 56 * 1024 * 1024


def _cparams(n_axes):
    return pltpu.CompilerParams(
        dimension_semantics=("arbitrary",) * n_axes, vmem_limit_bytes=_VMEM_LIMIT)


_NORM_T = 512


def _prenorm_kernel(x_ref, g_ref, w_ref, h_ref, px_ref, wb_ref):
    @pl.when(pl.program_id(0) == 0)
    def _():
        wb_ref[...] = w_ref[...].astype(BF16)

    x = x_ref[...]
    ms = jnp.mean(x * x, axis=-1, keepdims=True)
    h = (x * lax.rsqrt(ms + NORM_EPS) * g_ref[...]).astype(BF16)
    h_ref[...] = h
    px_ref[...] = jnp.dot(h, wb_ref[...], preferred_element_type=F32).astype(BF16)


def _prenorm(x2d, g, w_in):
    M, D = x2d.shape
    T = _NORM_T
    return pl.pallas_call(
        _prenorm_kernel,
        grid=(M // T,),
        in_specs=[pl.BlockSpec((T, D), lambda i: (i, 0)),
                  pl.BlockSpec((1, D), lambda i: (0, 0)),
                  pl.BlockSpec((D, D_RNN), lambda i: (0, 0), pipeline_mode=pl.Buffered(1))],
        out_specs=[pl.BlockSpec((T, D), lambda i: (i, 0)),
                   pl.BlockSpec((T, D_RNN), lambda i: (i, 0))],
        out_shape=[jax.ShapeDtypeStruct((M, D), BF16), jax.ShapeDtypeStruct((M, D_RNN), BF16)],
        scratch_shapes=[pltpu.VMEM((D, D_RNN), BF16)],
        compiler_params=_cparams(1),
        name="prenorm_proj_x",
    )(x2d, g.reshape(1, D), w_in)


def _cast_weight_once(w_ref, wb_ref):
    @pl.when(pl.program_id(1) == 0)
    def _():
        wb_ref[...] = w_ref[...].astype(BF16)


_CAST_ROWS = 256


def _proj_nat_kernel(lhs_ref, w_ref, *rest, cast_steps):
    n = len(cast_steps)
    wf_refs, o_ref, wo_refs, wb_ref = rest[:n], rest[n], rest[n + 1:2 * n + 1], rest[2 * n + 1]
    _cast_weight_once(w_ref, wb_ref)
    step = pl.program_id(0) * pl.num_programs(1) + pl.program_id(1)
    for wf_ref, wo_ref, steps in zip(wf_refs, wo_refs, cast_steps):
        @pl.when(step < steps)
        def _(wf_ref=wf_ref, wo_ref=wo_ref):
            wo_ref[...] = wf_ref[...].astype(BF16)
    o_ref[...] = jnp.dot(lhs_ref[...], wb_ref[...], preferred_element_type=F32).astype(BF16)


def _proj_nat(h, w_in, out_weights):
    M, K = h.shape
    tm, tn = 1024, _COL_BLK
    row_tiles = M // tm
    cast_steps = tuple(w.shape[0] // _CAST_ROWS for w in out_weights)
    assert max(cast_steps) <= _NAT_BLOCKS * row_tiles

    def chunk_spec(w, steps):
        return pl.BlockSpec((_CAST_ROWS, w.shape[1]),
                            lambda j, i: (jnp.minimum(j * row_tiles + i, steps - 1), 0))

    return pl.pallas_call(
        functools.partial(_proj_nat_kernel, cast_steps=cast_steps),
        grid=(_NAT_BLOCKS, row_tiles),
        in_specs=[pl.BlockSpec((tm, K), lambda j, i: (i, 0)),
                  pl.BlockSpec((K, tn), lambda j, i: (0, _nat_w_block(j)))]
        + [chunk_spec(w, n) for w, n in zip(out_weights, cast_steps)],
        out_specs=[pl.BlockSpec((tm, tn), lambda j, i: (i, j))]
        + [chunk_spec(w, n) for w, n in zip(out_weights, cast_steps)],
        out_shape=[jax.ShapeDtypeStruct((M, _NAT_BLOCKS * tn), BF16)]
        + [jax.ShapeDtypeStruct(w.shape, BF16) for w in out_weights],
        scratch_shapes=[pltpu.VMEM((K, tn), BF16)],
        compiler_params=_cparams(2),
        name="proj_nat",
    )(h, w_in, *out_weights)


_MAX_ROW_STRIDE = 4


def _proj_qkv_kernel(lhs_ref, w_ref, cos_ref, sa_ref, sb_ref, o_ref, wb_ref, acc_scr, *rest,
                     row_tiles, n_tiles, dil):
    t = pl.program_id(0)
    tm = acc_scr.shape[2]
    rows = tm // dil

    def class_rows(acc_prev, c):
        if dil == 1:
            return {0: acc_prev[c]}
        if dil <= _MAX_ROW_STRIDE:
            return {r: acc_prev[c, pl.ds(r, rows, stride=dil), :] for r in range(dil)}
        (tmp_scr,) = rest
        s0 = _MAX_ROW_STRIDE
        s1 = dil // s0
        assert s1 <= _MAX_ROW_STRIDE
        for r0 in range(s0):
            tmp_scr[c, r0] = acc_prev[c, pl.ds(r0, tm // s0, stride=s0), :]
        return {s0 * r1 + r0: tmp_scr[c, r0, pl.ds(r1, rows, stride=s1), :]
                for r0 in range(s0) for r1 in range(s1)}

    @pl.when(t == 0)
    def _():
        acc_scr[...] = jnp.zeros(acc_scr.shape, F32)

    @pl.when((t % row_tiles == 0) & (t < n_tiles))
    def _():
        wb_ref[...] = w_ref[...].astype(BF16)

    def body(acc_cur, acc_prev):
        is_qk = (t - 1) // row_tiles < 2
        cos = jnp.where(is_qk, cos_ref[...], 1.0)
        sa = jnp.where(is_qk, sa_ref[...], 0.0)
        sb = jnp.where(is_qk, sb_ref[...], 0.0)
        for c0 in range(0, HEADS_PER_GROUP, _HEADS_PER_DOT):
            lanes = slice(c0 * HEAD_DIM, (c0 + _HEADS_PER_DOT) * HEAD_DIM)
            if acc_cur is not None:
                acc = jnp.dot(lhs_ref[...], wb_ref[:, lanes], preferred_element_type=F32)
                for c in range(c0, c0 + _HEADS_PER_DOT):
                    acc_cur[c] = acc[:, (c - c0) * HEAD_DIM:(c - c0 + 1) * HEAD_DIM]
            for c in range(c0, c0 + _HEADS_PER_DOT):
                for r, v in class_rows(acc_prev, c).items():
                    rot = (v * cos[r] + pltpu.roll(v, HEAD_DIM - ROT_DIM // 2, 1) * sa[r]
                           + pltpu.roll(v, ROT_DIM // 2, 1) * sb[r])
                    o_ref[0, 0, c, r] = rot.astype(BF16)

    for parity in range(2):
        pl.when((t % 2 == parity) & (t < n_tiles))(
            functools.partial(body, acc_scr.at[parity], acc_scr.at[1 - parity]))
    pl.when(t == n_tiles)(functools.partial(body, None, acc_scr.at[(n_tiles - 1) % 2]))


def _proj_qkv(h, w_in, tables, group, B, S):
    M, K = h.shape
    dil = DILATIONS[group]
    L = S // dil
    tm, tn = 1024, _COL_BLK
    spt = S // tm
    row_tiles = M // tm
    n_tiles = 3 * row_tiles
    mm = lambda t: jnp.minimum(t, n_tiles - 1)
    ep = lambda t: jnp.maximum(t - 1, 0)
    tbl_spec = pl.BlockSpec((dil, tm // dil, HEAD_DIM), lambda t: (0, ep(t) % row_tiles % spt, 0))
    out = pl.pallas_call(
        functools.partial(_proj_qkv_kernel, row_tiles=row_tiles, n_tiles=n_tiles, dil=dil),
        grid=(n_tiles + 1,),
        in_specs=[pl.BlockSpec((tm, K), lambda t: (mm(t) % row_tiles, 0)),
                  pl.BlockSpec((K, tn),
                               lambda t: (0, _Q_BLK0 + group + 3 * (mm(t) // row_tiles))),
                  tbl_spec, tbl_spec, tbl_spec],
        out_specs=pl.BlockSpec(
            (1, 1, HEADS_PER_GROUP, dil, tm // dil, HEAD_DIM),
            lambda t: (ep(t) // row_tiles, ep(t) % row_tiles // spt, 0, 0,
                       ep(t) % row_tiles % spt, 0)),
        out_shape=jax.ShapeDtypeStruct((3, B, HEADS_PER_GROUP, dil, L, HEAD_DIM), BF16),
        scratch_shapes=[pltpu.VMEM((K, tn), BF16),
                        pltpu.VMEM((2, HEADS_PER_GROUP, tm, HEAD_DIM), F32)]
        + ([pltpu.VMEM((HEADS_PER_GROUP, _MAX_ROW_STRIDE, tm // _MAX_ROW_STRIDE, HEAD_DIM), F32)]
           if dil > _MAX_ROW_STRIDE else []),
        compiler_params=_cparams(1),
        name=f"proj_qkv_g{group}",
    )(h, w_in, *[tbl.reshape(dil, L, HEAD_DIM) for tbl in tables])
    return out.reshape(3, B, HEADS_PER_GROUP, S, HEAD_DIM)


def _rope_tables(S, dilation):
    half = ROT_DIM // 2
    inv = np.exp(-math.log(ROPE_THETA) * np.arange(half, dtype=np.float32)
                 * np.float32(2.0 / ROT_DIM)).astype(np.float32)
    L = S // dilation
    p = np.arange(S)
    pos = (p % L) * dilation + p // L
    ang = pos.astype(np.float32)[:, None] * inv[None, :]
    cos, sin = np.cos(ang).astype(np.float32), np.sin(ang).astype(np.float32)
    cos_t = np.ones((S, HEAD_DIM), np.float32)
    sin_a = np.zeros((S, HEAD_DIM), np.float32)
    sin_b = np.zeros((S, HEAD_DIM), np.float32)
    cos_t[:, :half] = cos
    cos_t[:, half:ROT_DIM] = cos
    sin_a[:, :half] = -sin
    sin_b[:, half:ROT_DIM] = sin
    return jnp.asarray(cos_t), jnp.asarray(sin_a), jnp.asarray(sin_b)


_RNN_TS = 256
_PAD = 8


def _softplus(z):
    return jnp.maximum(z, 0.0) + jnp.log1p(jnp.exp(-jnp.abs(z)))


def _scan8(a, b, row):
    for k in (1, 2, 4):
        a_sh = jnp.where(row >= k, pltpu.roll(a, k, 0), 1.0)
        b_sh = jnp.where(row >= k, pltpu.roll(b, k, 0), 0.0)
        b = a * b_sh + b
        a = a * a_sh
    return a, b


_RNN_PHASES = 4


def _rglru_gates(c, x_ref, cw_ref, cb_ref, wrg_ref, wig_ref, xpad, stage):
    ts, nph = _RNN_TS, _RNN_PHASES
    q = ts // nph
    W = RNN_BLOCK_W
    cs = slice(c * W, (c + 1) * W)
    xpad[c, _PAD:_PAD + ts, :] = x_ref[:, cs].astype(F32)
    xs = {off: xpad[c, pl.ds(_PAD + off, q, stride=nph), :]
          for off in range(-(CONV_W - 1), nph)}
    w = [cw_ref[k:k + 1, cs] for k in range(CONV_W)]
    parts = []
    for j in range(nph):
        acc = cb_ref[:, cs] + w[0] * xs[j - (CONV_W - 1)]
        for k in range(1, CONV_W):
            acc = acc + w[k] * xs[j - (CONV_W - 1) + k]
        parts.append(acc)
    xc = jnp.concatenate(parts, axis=0)
    xb = xc.astype(BF16)
    xpad[c, 0:_PAD, :] = xpad[c, ts:ts + _PAD, :]
    stage[0] = xc
    stage[1] = jnp.dot(xb, wrg_ref[c].astype(BF16), preferred_element_type=F32)
    stage[2] = jnp.dot(xb, wig_ref[c].astype(BF16), preferred_element_type=F32)


def _rglru_scan(c, stage, gate_ref, brg_ref, big_ref, lam_ref, o_ref, h_buf, carry):
    ts, nph = _RNN_TS, _RNN_PHASES
    q = ts // nph
    W = RNN_BLOCK_W
    row = lax.broadcasted_iota(jnp.int32, (8, W), 0)
    cs = slice(c * W, (c + 1) * W)
    xc = stage[0]
    r = jax.nn.sigmoid(stage[1] + brg_ref[:, cs])
    ig = jax.nn.sigmoid(stage[2] + big_ref[:, cs])
    log_a = (-LRU_C * _softplus(-lam_ref[:, cs])) * r
    a = jnp.exp(log_a)
    om = 1.0 - a * a
    stage[1] = a
    stage[2] = jnp.where(om > 0.0, om * lax.rsqrt(om), 0.0) * (ig * xc)
    part = lambda k, j: stage[k, j * q:(j + 1) * q, :]

    tot_a, tot_b = part(1, 0), part(2, 0)
    for j in range(1, nph):
        tot_b = part(1, j) * tot_b + part(2, j)
        tot_a = part(1, j) * tot_a
    h_in = carry[c]
    seg_in = []
    for m in range(q // 8):
        pa, pb = _scan8(tot_a[m * 8:(m + 1) * 8], tot_b[m * 8:(m + 1) * 8], row)
        after = pa * h_in + pb
        seg_in.append(jnp.where(row >= 1, pltpu.roll(after, 1, 0), h_in))
        h_in = after[7:8, :]
    carry[c] = h_in
    h = jnp.concatenate(seg_in, axis=0)
    for j in range(nph):
        h = part(1, j) * h + part(2, j)
        h_buf[pl.ds(j, q, stride=nph), :] = h
    g = gate_ref[:, cs].astype(F32)
    o_ref[:, cs] = (h_buf[...] * (g * jax.nn.sigmoid(g))).astype(BF16)


_SCALE = HEAD_DIM ** -0.5
_ATTN_BATCH = 16


def _attend(q, k, v, has_prev):
    n, nk, _ = k.shape
    s = jnp.einsum("nqd,nkd->nqk", q, k, preferred_element_type=F32)
    qi = lax.broadcasted_iota(jnp.int32, (1, ATTN_BLOCK, nk), 1)
    kj = lax.broadcasted_iota(jnp.int32, (1, ATTN_BLOCK, nk), 2)
    if has_prev:
        d = kj - qi
        valid = (d >= 0) & (d <= ATTN_BLOCK)
    else:
        valid = kj <= qi
    s = jnp.where(valid, s, MASK_VALUE)
    m = jnp.max(s, axis=-1, keepdims=True)
    p = jnp.exp2((s - m) * (_SCALE * math.log2(math.e)))
    v_ones = jnp.concatenate([v, jnp.ones_like(v)], axis=-1)
    oe = jnp.einsum("nqk,nkd->nqd", p.astype(BF16), v_ones, preferred_element_type=F32)
    l = oe[:, :, HEAD_DIM:]
    o = oe[:, :, :HEAD_DIM] * (1.0 / l)
    return o, m * _SCALE + jnp.log(l)


def _attn_kernel(g0_ref, g1_ref, g2_ref, gate_ref, out_ref, o_scr, ld_scr):
    S = o_scr.shape[1]
    blk = ATTN_BLOCK

    def run_blocks(ref, bases, has_prev):
        res = []
        for i in range(0, len(bases), _ATTN_BATCH):
            chunk = bases[i:i + _ATTN_BATCH]
            lo = blk if has_prev else 0
            q = jnp.stack([ref[0, 0, 0, b:b + blk, :] for b in chunk])
            k = jnp.stack([ref[1, 0, 0, b - lo:b + blk, :] for b in chunk])
            v = jnp.stack([ref[2, 0, 0, b - lo:b + blk, :] for b in chunk])
            o, ld = _attend(q, k, v, has_prev)
            res += [(o[j], ld[j]) for j in range(len(chunk))]
        return res

    for gi, (ref, dil) in enumerate(zip((g0_ref, g1_ref, g2_ref), DILATIONS)):
        L = S // dil
        for has_prev in (False, True):
            blocks = [(r, n) for r in range(dil) for n in range(L // blk) if (n > 0) == has_prev]
            if not blocks:
                continue
            outs = run_blocks(ref, [r * L + n * blk for r, n in blocks], has_prev)
            for (r, n), (o, ld) in zip(blocks, outs):
                if dil == 1:
                    rows = pl.ds(n * blk, blk)
                else:
                    rows = pl.ds(n * blk * dil + r, blk, stride=dil)
                o_scr[gi, rows, :] = o
                ld_scr[gi, rows, :] = ld

    ch = 256

    def merge(c, _):
        rows = pl.ds(pl.multiple_of(c * ch, ch), ch)
        l0, l1, l2 = ld_scr[0, rows, :], ld_scr[1, rows, :], ld_scr[2, rows, :]
        m = jnp.maximum(jnp.maximum(l0, l1), l2)
        e0, e1, e2 = jnp.exp(l0 - m), jnp.exp(l1 - m), jnp.exp(l2 - m)
        num = e0 * o_scr[0, rows, :] + e1 * o_scr[1, rows, :] + e2 * o_scr[2, rows, :]
        o = num * (1.0 / (e0 + e1 + e2))
        g = gate_ref[0, rows, :].astype(F32)
        out_ref[0, rows, :] = (o * (g * jax.nn.sigmoid(g))).astype(BF16)
        return 0

    lax.fori_loop(0, S // ch, merge, 0)


def _attention(qkv, p_nat, B, S):
    H = HEADS_PER_GROUP
    qkv_spec = lambda: pl.BlockSpec((3, 1, 1, S, HEAD_DIM), lambda b, h: (0, b, h, 0, 0))
    return pl.pallas_call(
        _attn_kernel,
        grid=(B, H),
        in_specs=[qkv_spec(), qkv_spec(), qkv_spec(),
                  pl.BlockSpec((1, S, HEAD_DIM),
                               lambda b, h: (b, 0, _P_GATE_BLK * _COL_BLK // HEAD_DIM + h))],
        out_specs=pl.BlockSpec((1, S, HEAD_DIM), lambda b, h: (b, 0, h)),
        out_shape=jax.ShapeDtypeStruct((B, S, H * HEAD_DIM), BF16),
        scratch_shapes=[pltpu.VMEM((3, S, HEAD_DIM), F32), pltpu.VMEM((3, S, HEAD_DIM), F32)],
        compiler_params=_cparams(2),
        name="dilated_attn",
    )(*qkv, p_nat)


_DOT_LANES = 256


def _tail_kernel(xr_ref, gr_ref, cw_ref, cb_ref, wrg_ref, brg_ref, wig_ref, big_ref, lam_ref,
                 u_attn_ref, g_rnn_ref, g_attn_ref, x_ref, w_rnn_ref, w_attn_ref, w_o_ref,
                 g_post_ref, o_ref, xpad, stage, h_scr, carry, u_scr, merged_scr, y_scr, *,
                 tiles_per_seq, n_tiles):
    s = pl.program_id(0)

    @pl.when((s % tiles_per_seq == 0) & (s < n_tiles))
    def _():
        xpad[:, 0:_PAD, :] = jnp.zeros((RNN_BLOCKS, _PAD, RNN_BLOCK_W), F32)
        carry[...] = jnp.zeros(carry.shape, F32)

    def body(u_next, u_cur):
        n_chunks = D_MODEL // _DOT_LANES
        gates = lambda c: _rglru_gates(c, xr_ref, cw_ref, cb_ref, wrg_ref, wig_ref, xpad,
                                       stage.at[c % 2])
        if u_next is not None:
            gates(0)
        for c in range(RNN_BLOCKS):
            if u_next is not None:
                if c + 1 < RNN_BLOCKS:
                    gates(c + 1)
                _rglru_scan(c, stage.at[c % 2], gr_ref, brg_ref, big_ref, lam_ref, u_next,
                            h_scr.at[c % 2], carry)
            if u_cur is None:
                continue
            lanes = slice((c % n_chunks) * _DOT_LANES, (c % n_chunks + 1) * _DOT_LANES)
            if c < n_chunks:
                y_rnn = jnp.dot(u_cur[...], w_rnn_ref[:, lanes], preferred_element_type=F32)
                y_attn = jnp.dot(u_attn_ref[...], w_attn_ref[:, lanes],
                                 preferred_element_type=F32)
                merged_scr[:, lanes] = (
                    jax.nn.sigmoid(g_rnn_ref[:, lanes].astype(F32)) * y_rnn
                    + jax.nn.sigmoid(g_attn_ref[:, lanes].astype(F32)) * y_attn).astype(BF16)
            else:
                y_scr[:, lanes] = jnp.dot(merged_scr[...], w_o_ref[:, lanes],
                                          preferred_element_type=F32)
        if u_cur is None:
            return
        y = y_scr[...]
        ms = jnp.mean(y * y, axis=-1, keepdims=True)
        o_ref[...] = x_ref[...] + y * lax.rsqrt(ms + NORM_EPS) * g_post_ref[...]

    for parity in range(2):
        pl.when((s % 2 == parity) & (s > 0) & (s < n_tiles))(
            functools.partial(body, u_scr.at[parity], u_scr.at[1 - parity]))
    pl.when(s == 0)(functools.partial(body, u_scr.at[0], None))
    pl.when(s == n_tiles)(functools.partial(body, None, u_scr.at[(n_tiles - 1) % 2]))


def _tail(p_x, p_nat, u_attn, x2d, conv_w, conv_b, w_rg, b_rg, w_ig, b_ig, lam,
          w_rnn_b, w_attn_b, w_o_b, g_post, S):
    M, D = x2d.shape
    tm, C, W = _RNN_TS, D_RNN, RNN_BLOCK_W
    n_tiles = M // tm
    assert RNN_BLOCKS == 2 * (D // _DOT_LANES)
    nxt = lambda s: jnp.minimum(s, n_tiles - 1)
    cur = lambda s: jnp.maximum(s - 1, 0)
    const = lambda shape: pl.BlockSpec(shape, lambda s: (0,) * len(shape))
    resident = lambda shape: pl.BlockSpec(shape, lambda s: (0, 0), pipeline_mode=pl.Buffered(1))
    return pl.pallas_call(
        functools.partial(_tail_kernel, tiles_per_seq=S // tm, n_tiles=n_tiles),
        grid=(n_tiles + 1,),
        in_specs=[pl.BlockSpec((tm, C), lambda s: (nxt(s), 0)),
                  pl.BlockSpec((tm, C), lambda s: (nxt(s), _P_RGATE_BLK * _COL_BLK // C)),
                  const((CONV_W, C)), const((1, C)), const((RNN_BLOCKS, W, W)), const((1, C)),
                  const((RNN_BLOCKS, W, W)), const((1, C)), const((1, C)),
                  pl.BlockSpec((tm, u_attn.shape[1]), lambda s: (cur(s), 0)),
                  pl.BlockSpec((tm, D), lambda s: (cur(s), _P_GRNN_BLK * _COL_BLK // D)),
                  pl.BlockSpec((tm, D), lambda s: (cur(s), _P_GATTN_BLK * _COL_BLK // D)),
                  pl.BlockSpec((tm, D), lambda s: (cur(s), 0)),
                  resident(w_rnn_b.shape), resident(w_attn_b.shape), resident(w_o_b.shape),
                  const((1, D))],
        out_specs=pl.BlockSpec((tm, D), lambda s: (cur(s), 0)),
        out_shape=jax.ShapeDtypeStruct((M, D), F32),
        scratch_shapes=[pltpu.VMEM((RNN_BLOCKS, tm + _PAD, W), F32),
                        pltpu.VMEM((2, 3, tm, W), F32),
                        pltpu.VMEM((2, tm, W), F32),
                        pltpu.VMEM((RNN_BLOCKS, 1, W), F32),
                        pltpu.VMEM((2, tm, C), BF16),
                        pltpu.VMEM((tm, D), BF16),
                        pltpu.VMEM((tm, D), F32)],
        compiler_params=_cparams(1),
        name="rglru_output",
    )(p_x, p_nat, conv_w, conv_b.reshape(1, C), w_rg, b_rg.reshape(1, C), w_ig,
      b_ig.reshape(1, C), lam.reshape(1, C), u_attn, p_nat, p_nat, x2d,
      w_rnn_b, w_attn_b, w_o_b, g_post.reshape(1, D))


def _layer(x, ln_pre_g, w_in, conv_w, conv_b, w_rg, b_rg, w_ig, b_ig, lru_lambda,
           w_rnn_out, w_attn_out, w_o, ln_post_g):
    B, S, D = x.shape
    M = B * S
    x2d = x.reshape(M, D)
    h, p_x = _prenorm(x2d, ln_pre_g, w_in)
    p_nat, w_rnn_b, w_attn_b, w_o_b = _proj_nat(h, w_in, (w_rnn_out, w_attn_out, w_o))
    qkv = [_proj_qkv(h, w_in, _rope_tables(S, dil), g, B, S) for g, dil in enumerate(DILATIONS)]
    u_attn = _attention(qkv, p_nat.reshape(B, S, p_nat.shape[1]), B, S)
    out = _tail(p_x, p_nat, u_attn.reshape(M, -1), x2d, conv_w, conv_b, w_rg, b_rg, w_ig, b_ig,
                lru_lambda, w_rnn_b, w_attn_b, w_o_b, ln_post_g, S)
    return out.reshape(B, S, D)


def kernel(x, ln_pre_g, w_in, conv_w, conv_b, w_rg, b_rg, w_ig, b_ig, lru_lambda,
           w_rnn_out, w_attn_out, w_o, ln_post_g):
    for layer in range(ln_pre_g.shape[0]):
        x = _layer(x, ln_pre_g[layer], w_in[layer], conv_w[layer], conv_b[layer], w_rg[layer],
                   b_rg[layer], w_ig[layer], b_ig[layer], lru_lambda[layer], w_rnn_out[layer],
                   w_attn_out[layer], w_o[layer], ln_post_g[layer])
    return x
```
